```python
import math, functools
import jax, jax.numpy as jnp
from jax import lax
import numpy as np

D_MODEL = 1024
BATCH = 16
SEQ = 2048
DEPTH = 1
DEC_BATCH = 128
DEC_SEQ = 8
PAST_LEN = 8192
PAGE_SIZE = 128

H_REC = 4
DK_REC = 128
DV_REC = 128
W_REC_K = H_REC * DK_REC
W_REC_V = H_REC * DV_REC
H_ATT = 4
DK_ATT = 64
DV_ATT = 2 * DK_ATT
W_ATT_QK = H_ATT * 2 * DK_ATT
W_ATT = H_ATT * DV_ATT
D_FF = 4 * D_MODEL
NUM_BUCKETS = 32
MAX_DISTANCE = 128
CHUNK = 32
Q_BLOCK = 128
EPS = 1e-6
COL_SIZES = (W_REC_K, W_REC_K, W_REC_V, W_REC_V, W_ATT_QK, W_ATT_QK, W_ATT, D_MODEL, D_MODEL)
IN_WIDTH = 4 * W_REC_K + 2 * W_ATT_QK + W_ATT + 2 * D_MODEL

kernel_name = 'hgrn2_diffattn_gated_hybrid_step'


def rms_norm(x, w):
    xf = x.astype(jnp.float32)
    return xf * lax.rsqrt(jnp.mean(xf * xf, axis=-1, keepdims=True) + EPS) * w.astype(jnp.float32)


def rel_bucket(dist):
    dist = jnp.maximum(dist, 0)
    max_exact = NUM_BUCKETS // 2
    large = max_exact + (jnp.log(jnp.maximum(dist, 1).astype(jnp.float32) / max_exact)
                         / math.log(MAX_DISTANCE / max_exact) * (NUM_BUCKETS - max_exact)).astype(jnp.int32)
    large = jnp.minimum(large, NUM_BUCKETS - 1)
    return jnp.where(dist < max_exact, dist, large)


def rel_bias_lookup(rel_bias, qpos, kpos):
    dist = qpos[:, None] - kpos[None, :]
    bias = rel_bias.astype(jnp.float32)[rel_bucket(dist)].transpose(2, 0, 1)
    return bias, dist >= 0


def split_columns(z):
    out, off = [], 0
    for s in COL_SIZES:
        out.append(z[..., off:off + s])
        off += s
    return out


def lambda_init(layer):
    return 0.8 - 0.6 * math.exp(-0.3 * layer)


def hgrn2_chunkwise(q, k, logf, v, s0):
    B, T, H, K = q.shape
    V = v.shape[-1]
    c = min(CHUNK, T)
    n = -(-T // c)
    pad = n * c - T
    if pad:
        pw = ((0, 0), (0, pad), (0, 0), (0, 0))
        q, k, logf, v = jnp.pad(q, pw), jnp.pad(k, pw), jnp.pad(logf, pw), jnp.pad(v, pw)

    def blocks(a):
        return a.reshape(B, n, c, H, a.shape[-1]).transpose(1, 0, 3, 2, 4)

    q, k, logf, v = blocks(q), blocks(k), blocks(logf), blocks(v)
    b = jnp.cumsum(logf, axis=3)
    b_last = b[:, :, :, -1:, :]
    qe = q * jnp.exp(b)
    ke = k * jnp.exp(-b)
    kd = k * jnp.exp(b_last - b)
    decay = jnp.exp(b[:, :, :, -1, :])
    causal = jnp.tril(jnp.ones((c, c), dtype=bool))
    a = jnp.where(causal, jnp.einsum('nbhck,nbhsk->nbhcs', qe, ke), 0.0)
    o_intra = jnp.einsum('nbhcs,nbhsv->nbhcv', a, v)

    def step(s, inp):
        qe_i, kd_i, v_i, dec_i = inp
        o = jnp.einsum('bhck,bhkv->bhcv', qe_i, s)
        s = dec_i[..., None] * s + jnp.einsum('bhck,bhcv->bhkv', kd_i, v_i)
        return s, o

    s_final, o_inter = lax.scan(step, s0, (qe, kd, v, decay))
    o = (o_intra + o_inter).transpose(1, 0, 3, 2, 4).reshape(B, n * c, H, V)[:, :T]
    return o, s_final


def diff_attn_prompt(q1, q2, k1, k2, v, lam, rel_bias):
    B, S, H, _ = q1.shape
    qb = min(Q_BLOCK, S)
    nb = S // qb
    scale = DK_ATT ** -0.5
    kpos = jnp.arange(S)

    def blk(a):
        return a.reshape(B, nb, qb, H, a.shape[-1]).transpose(1, 0, 2, 3, 4)

    def one_block(inp):
        i, q1b, q2b = inp
        bias, mask = rel_bias_lookup(rel_bias, i * qb + jnp.arange(qb), kpos)

        def probs(qq, kk):
            lg = jnp.einsum('bqhd,bkhd->bhqk', qq, kk) * scale + bias
            return jax.nn.softmax(jnp.where(mask, lg, -jnp.inf), axis=-1)

        w = probs(q1b, k1) - lam * probs(q2b, k2)
        return jnp.einsum('bhqk,bkhv->bqhv', w, v)

    out = lax.map(one_block, (jnp.arange(nb), blk(q1), blk(q2)))
    return out.transpose(1, 0, 2, 3, 4).reshape(B, S, H, -1)


def diff_attn_sample(q1, q2, k1, k2, v, lam, rel_bias, cache_k, cache_v, layer, page_table):
    B, T, H, _ = q1.shape
    n_pages = page_table.shape[1]
    past = n_pages * PAGE_SIZE
    scale = DK_ATT ** -0.5
    qpos = past + jnp.arange(T)
    bias_new, mask_new = rel_bias_lookup(rel_bias, qpos, qpos)

    def init(qq, kk):
        lg = jnp.where(mask_new, jnp.einsum('bqhd,bkhd->bhqk', qq, kk) * scale + bias_new, -jnp.inf)
        m = lg.max(-1)
        p = jnp.exp(lg - m[..., None])
        return (m, p.sum(-1), jnp.einsum('bhqk,bkhv->bhqv', p, v))

    def page_step(carry, inp):
        j, pt = inp
        kp = cache_k[layer, pt].astype(jnp.float32)
        vp = cache_v[layer, pt].astype(jnp.float32)
        bias_p, _ = rel_bias_lookup(rel_bias, qpos, j * PAGE_SIZE + jnp.arange(PAGE_SIZE))

        def upd(st, qq, kk):
            m, l, acc = st
            lg = jnp.einsum('bqhd,bkhd->bhqk', qq, kk) * scale + bias_p
            m_new = jnp.maximum(m, lg.max(-1))
            corr = jnp.exp(m - m_new)
            p = jnp.exp(lg - m_new[..., None])
            return (m_new, l * corr + p.sum(-1), acc * corr[..., None] + jnp.einsum('bhqk,bkhv->bhqv', p, vp))

        return (upd(carry[0], q1, kp[..., :DK_ATT]), upd(carry[1], q2, kp[..., DK_ATT:])), None

    (s1, s2), _ = lax.scan(page_step, (init(q1, k1), init(q2, k2)), (jnp.arange(n_pages), page_table.T))
    w = s1[2] / s1[1][..., None] - lam * s2[2] / s2[1][..., None]
    return w.transpose(0, 2, 1, 3)


def decoder_layer(x, s0, attend, lb, lam, lam_init, lp):
    (n1, w_in, hgrn_w, qn, kn, sub_w, wa, wb, wo, n2, wu, wd) = lp
    B, T, _ = x.shape
    xn = rms_norm(x, n1)
    z = xn @ w_in
    hq, hf, hi, hg, aq, ak, av, ga, gb = split_columns(z)
    q = jax.nn.silu(hq).reshape(B, T, H_REC, DK_REC)
    f = lb + (1.0 - lb) * jax.nn.sigmoid(hf)
    logf = jnp.log(f).reshape(B, T, H_REC, DK_REC)
    kk = (1.0 - f).reshape(B, T, H_REC, DK_REC)
    vv = hi.reshape(B, T, H_REC, DV_REC)
    o_rec, s_new = hgrn2_chunkwise(q, kk, logf, vv, s0.astype(jnp.float32))
    y_rec = rms_norm(o_rec, hgrn_w).reshape(B, T, W_REC_V) * jax.nn.silu(hg)
    qa = rms_norm(aq.reshape(B, T, H_ATT, 2, DK_ATT), qn)
    ka = rms_norm(ak.reshape(B, T, H_ATT, 2, DK_ATT), kn)
    va = av.reshape(B, T, H_ATT, DV_ATT)
    o_att = attend(qa[..., 0, :], qa[..., 1, :], ka[..., 0, :], ka[..., 1, :], va, lam)
    y_att = (rms_norm(o_att, sub_w) * (1.0 - lam_init)).reshape(B, T, W_ATT)
    mixed = jax.nn.sigmoid(ga) * (y_rec @ wa) + jax.nn.sigmoid(gb) * (y_att @ wb)
    h = x + (mixed @ wo).astype(x.dtype)
    hn = rms_norm(h, n2)
    y = h + (jnp.square(jax.nn.relu(hn @ wu)) @ wd).astype(x.dtype)
    return y, ka.reshape(B, T, H_ATT, 2 * DK_ATT), va, s_new


def setup_inputs(seed: int = 0) -> dict:
    key = jax.random.key(seed)
    ks = jax.random.split(key, 32)
    f32 = jnp.float32
    n_pages = PAST_LEN // PAGE_SIZE
    n_used = DEC_BATCH * n_pages
    n_pool = n_used + max(1, n_used // 4)

    def nrm(k, shape, scale):
        return jax.random.normal(k, shape, f32) * scale

    def gain(k, shape):
        return 1.0 + nrm(k, shape, 0.02)

    perm = jax.random.permutation(ks[5], n_pool)
    page_table = perm[:n_used].reshape(DEC_BATCH, n_pages).astype(jnp.int32)
    return {
        'x_prompt': nrm(ks[0], (BATCH, SEQ, D_MODEL), 1.0),
        'x_sample': nrm(ks[1], (DEC_BATCH, DEC_SEQ, D_MODEL), 1.0),
        'cache_k': nrm(ks[2], (DEPTH, n_pool, PAGE_SIZE, H_ATT, 2 * DK_ATT), 1.0),
        'cache_v': nrm(ks[3], (DEPTH, n_pool, PAGE_SIZE, H_ATT, DV_ATT), 1.0),
        'state_hgrn': nrm(ks[4], (DEPTH, DEC_BATCH, H_REC, DK_REC, DV_REC), 0.5),
        'page_table': page_table,
        'norm1_w': gain(ks[6], (DEPTH, D_MODEL)),
        'w_in': nrm(ks[7], (DEPTH, D_MODEL, IN_WIDTH), D_MODEL ** -0.5),
        'lower_bounds': nrm(ks[8], (DEPTH + 1, W_REC_K), 0.1),
        'hgrn_norm_w': gain(ks[9], (DEPTH, DV_REC)),
        'q_norm_w': gain(ks[10], (DEPTH, 2, DK_ATT)),
        'k_norm_w': gain(ks[11], (DEPTH, 2, DK_ATT)),
        'lam_q1': nrm(ks[12], (DEPTH, DK_ATT), 0.1),
        'lam_k1': nrm(ks[13], (DEPTH, DK_ATT), 0.1),
        'lam_q2': nrm(ks[14], (DEPTH, DK_ATT), 0.1),
        'lam_k2': nrm(ks[15], (DEPTH, DK_ATT), 0.1),
        'subln_w': gain(ks[16], (DEPTH, DV_ATT)),
        'rel_bias': nrm(ks[17], (NUM_BUCKETS, H_ATT), 0.5),
        'w_branch_a': nrm(ks[18], (DEPTH, W_REC_V, D_MODEL), W_REC_V ** -0.5),
        'w_branch_b': nrm(ks[19], (DEPTH, W_ATT, D_MODEL), W_ATT ** -0.5),
        'w_out': nrm(ks[20], (DEPTH, D_MODEL, D_MODEL), D_MODEL ** -0.5),
        'norm2_w': gain(ks[21], (DEPTH, D_MODEL)),
        'w_up': nrm(ks[22], (DEPTH, D_MODEL, D_FF), D_MODEL ** -0.5),
        'w_down': nrm(ks[23], (DEPTH, D_FF, D_MODEL), D_FF ** -0.5),
    }


def reference(x_prompt, x_sample, cache_k, cache_v, state_hgrn, page_table, norm1_w, w_in, lower_bounds,
              hgrn_norm_w, q_norm_w, k_norm_w, lam_q1, lam_k1, lam_q2, lam_k2, subln_w, rel_bias,
              w_branch_a, w_branch_b, w_out, norm2_w, w_up, w_down):
    f32 = jnp.float32
    lb_all = jnp.cumsum(jax.nn.softmax(lower_bounds.astype(f32), axis=0), axis=0)
    hp, hs = x_prompt, x_sample
    kp_l, vp_l, sp_l, ks_l, vs_l, ss_l = [], [], [], [], [], []
    for l in range(DEPTH):
        lam_init = lambda_init(l)
        lam = (jnp.exp(jnp.dot(lam_q1[l].astype(f32), lam_k1[l].astype(f32)))
               - jnp.exp(jnp.dot(lam_q2[l].astype(f32), lam_k2[l].astype(f32))) + lam_init)
        lp = (norm1_w[l], w_in[l], hgrn_norm_w[l], q_norm_w[l], k_norm_w[l], subln_w[l],
              w_branch_a[l], w_branch_b[l], w_out[l], norm2_w[l], w_up[l], w_down[l])
        s0p = jnp.zeros((hp.shape[0], H_REC, DK_REC, DV_REC), f32)
        hp, kp, vp, sp = decoder_layer(hp, s0p, functools.partial(diff_attn_prompt, rel_bias=rel_bias),
                                       lb_all[l], lam, lam_init, lp)
        attend_s = functools.partial(diff_attn_sample, rel_bias=rel_bias, cache_k=cache_k, cache_v=cache_v,
                                     layer=l, page_table=page_table)
        hs, ks_, vs_, ss_ = decoder_layer(hs, state_hgrn[l], attend_s, lb_all[l], lam, lam_init, lp)
        kp_l.append(kp.astype(cache_k.dtype))
        vp_l.append(vp.astype(cache_v.dtype))
        sp_l.append(sp.astype(state_hgrn.dtype))
        ks_l.append(ks_.astype(cache_k.dtype))
        vs_l.append(vs_.astype(cache_v.dtype))
        ss_l.append(ss_.astype(state_hgrn.dtype))
    y_prompt, y_sample = hp, hs
    k_prompt, v_prompt, s_prompt = jnp.stack(kp_l), jnp.stack(vp_l), jnp.stack(sp_l)
    k_sample, v_sample, s_sample = jnp.stack(ks_l), jnp.stack(vs_l), jnp.stack(ss_l)
    return (y_prompt, y_sample, k_prompt, v_prompt, s_prompt, k_sample, v_sample, s_sample)
```

```python
import functools
import math

import jax
import jax.numpy as jnp
from jax import lax
from jax.experimental import pallas as pl
from jax.experimental.pallas import tpu as pltpu

F32 = jnp.float32
BF16 = jnp.bfloat16

D_MODEL = 1024
H_REC, DK_REC, DV_REC = 4, 128, 128
W_REC = H_REC * DK_REC
H_ATT, DK_ATT = 4, 64
DV_ATT = 2 * DK_ATT
W_ATT = H_ATT * DV_ATT
D_FF = 4 * D_MODEL
NUM_BUCKETS = 32
MAX_DISTANCE = 128
PAGE_SIZE = 128
HGRN_CHUNK = 32
EPS = 1e-6
NEG = -1e30

OFF_HQ, OFF_HF, OFF_HI, OFF_HG = 0, W_REC, 2 * W_REC, 3 * W_REC
OFF_AQ = 4 * W_REC
OFF_AK = OFF_AQ + W_ATT
OFF_AV = OFF_AK + W_ATT
OFF_GA = OFF_AV + W_ATT
OFF_GB = OFF_GA + D_MODEL
IN_WIDTH = OFF_GB + D_MODEL

V7X_VMEM_LIMIT_BYTES = 56 * 1024 * 1024
PAGES_PER_STEP = 8

_NT = (((1,), (1,)), ((), ()))


def _const_spec(shape):
    n = len(shape)
    return pl.BlockSpec(shape, lambda *_: (0,) * n, pipeline_mode=pl.Buffered(1))


def _sigmoid(x):
    return 1.0 / (1.0 + jnp.exp(-x))


def _in_proj_kernel(x_ref, n1_ref, w_ref, lb_ref, qn_ref, kn_ref, pseg_ref,
                    q_ref, lf_ref, vr_ref, g_ref, qa_ref, k32_ref, k16_ref, v32_ref, v16_ref,
                    ga_ref, gb_ref):
    x = x_ref[...]
    ms = jnp.mean(x * x, axis=-1, keepdims=True)
    xb = (x * lax.rsqrt(ms + EPS) * n1_ref[...]).astype(BF16)

    def proj(off, size):
        return jnp.dot(xb, w_ref[:, off:off + size], preferred_element_type=F32)

    def seg_rms(z):
        z2 = z * z
        hi = z2.astype(BF16)
        lo = (z2 - hi.astype(F32)).astype(BF16)
        p = pseg_ref[...]
        m = jnp.dot(hi, p, preferred_element_type=F32) + jnp.dot(lo, p, preferred_element_type=F32)
        return z * lax.rsqrt(m + EPS)

    hq = proj(OFF_HQ, W_REC)
    q_ref[...] = (hq * _sigmoid(hq)).astype(BF16)
    hf = proj(OFF_HF, W_REC)
    lb = lb_ref[...]
    lf_ref[...] = jnp.log(lb + (1.0 - lb) * _sigmoid(hf))
    vr_ref[...] = proj(OFF_HI, W_REC).astype(BF16)
    hg = proj(OFF_HG, W_REC)
    g_ref[...] = (hg * _sigmoid(hg)).astype(BF16)

    qa_ref[...] = (seg_rms(proj(OFF_AQ, W_ATT)) * qn_ref[...]).astype(BF16)
    ka = seg_rms(proj(OFF_AK, W_ATT)) * kn_ref[...]
    k32_ref[...] = ka
    k16_ref[...] = ka.astype(BF16)
    va = proj(OFF_AV, W_ATT)
    v32_ref[...] = va
    v16_ref[...] = va.astype(BF16)
    ga_ref[...] = _sigmoid(proj(OFF_GA, D_MODEL)).astype(BF16)
    gb_ref[...] = _sigmoid(proj(OFF_GB, D_MODEL)).astype(BF16)


def _in_proj(x2d, n1, w_in16, lb, qn_row, kn_row, pseg, tm):
    m = x2d.shape[0]
    row = lambda w: pl.BlockSpec((tm, w), lambda i: (i, 0))
    outs = [(W_REC, BF16), (W_REC, F32), (W_REC, BF16), (W_REC, BF16), (W_ATT, BF16), (W_ATT, F32),
            (W_ATT, BF16), (W_ATT, F32), (W_ATT, BF16), (D_MODEL, BF16), (D_MODEL, BF16)]
    return pl.pallas_call(
        _in_proj_kernel,
        grid=(m // tm,),
        in_specs=[row(D_MODEL), _const_spec((1, D_MODEL)), _const_spec((D_MODEL, IN_WIDTH)),
                  _const_spec((1, W_REC)), _const_spec((1, W_ATT)), _const_spec((1, W_ATT)),
                  _const_spec((W_ATT, W_ATT))],
        out_specs=[row(w) for w, _ in outs],
        out_shape=[jax.ShapeDtypeStruct((m, w), d) for w, d in outs],
        compiler_params=pltpu.CompilerParams(dimension_semantics=("parallel",),
                                             vmem_limit_bytes=V7X_VMEM_LIMIT_BYTES),
        name="in_proj",
    )(x2d, n1, w_in16, lb, qn_row, kn_row, pseg)


def _hgrn_kernel(q_ref, lf_ref, v_ref, g_ref, s0_ref, w_ref, y_ref, s_ref, st_scr, o_scr, *, bb, tc, c):
    t = pl.program_id(2)
    mm = BF16 if c >= 16 else F32

    @pl.when(t == 0)
    def _():
        for i in range(bb):
            st_scr[i] = s0_ref[i, 0].T

    r_i = lax.broadcasted_iota(jnp.int32, (c, c), 0)
    c_i = lax.broadcasted_iota(jnp.int32, (c, c), 1)
    causal = r_i >= c_i
    tril = causal.astype(F32)

    for i in range(bb):
        def chunk(n, st):
            rows = pl.ds(pl.multiple_of(n * c, c), c)
            lf = lf_ref[i, rows, :]
            q = q_ref[i, rows, :].astype(F32)
            v = v_ref[i, rows, :].astype(F32)
            b = jnp.dot(tril, lf, preferred_element_type=F32, precision=lax.Precision.HIGHEST)
            b_last = b[c - 1:c, :]
            k = 1.0 - jnp.exp(lf)
            qe = (q * jnp.exp(b)).astype(mm)
            ke = (k * jnp.exp(-b)).astype(mm)
            kd = (k * jnp.exp(b_last - b)).astype(mm)
            a = lax.dot_general(qe, ke, _NT, preferred_element_type=F32)
            a = jnp.where(causal, a, 0.0)
            o = jnp.dot(a.astype(mm), v.astype(mm), preferred_element_type=F32)
            o = o + lax.dot_general(qe, st.astype(mm), _NT, preferred_element_type=F32)
            o_scr[rows, :] = o
            upd = jnp.dot(v.T.astype(mm), kd, preferred_element_type=F32)
            return st * jnp.exp(b_last) + upd

        st = lax.fori_loop(0, tc // c, chunk, st_scr[i])
        st_scr[i] = st
        o = o_scr[...]
        on = o * lax.rsqrt(jnp.mean(o * o, axis=-1, keepdims=True) + EPS) * w_ref[...]
        y_ref[i] = (on * g_ref[i].astype(F32)).astype(BF16)

    @pl.when(t == pl.num_programs(2) - 1)
    def _():
        for i in range(bb):
            s_ref[i, 0] = st_scr[i].T


def _hgrn(q, lf, v, g, s0, w_row, bb, tc):
    b, t, _ = q.shape
    c = min(HGRN_CHUNK, t)
    blk = pl.BlockSpec((bb, tc, DK_REC), lambda bi, h, ti: (bi, ti, h))
    sblk = pl.BlockSpec((bb, 1, DK_REC, DV_REC), lambda bi, h, ti: (bi, h, 0, 0))
    return pl.pallas_call(
        functools.partial(_hgrn_kernel, bb=bb, tc=tc, c=c),
        grid=(b // bb, H_REC, t // tc),
        in_specs=[blk, blk, blk, blk, sblk, _const_spec((1, DV_REC))],
        out_specs=[blk, sblk],
        out_shape=[jax.ShapeDtypeStruct((b, t, W_REC), BF16),
                   jax.ShapeDtypeStruct((b, H_REC, DK_REC, DV_REC), F32)],
        scratch_shapes=[pltpu.VMEM((bb, DV_REC, DK_REC), F32), pltpu.VMEM((tc, DV_REC), F32)],
        compiler_params=pltpu.CompilerParams(dimension_semantics=("parallel", "parallel", "arbitrary")),
        name="hgrn",
    )(q, lf, v, g, s0, w_row)


def _rel_bucket(dist):
    dist = jnp.maximum(dist, 0)
    max_exact = NUM_BUCKETS // 2
    large = max_exact + (jnp.log(jnp.maximum(dist, 1).astype(F32) / max_exact)
                         / math.log(MAX_DISTANCE / max_exact) * (NUM_BUCKETS - max_exact)).astype(jnp.int32)
    large = jnp.minimum(large, NUM_BUCKETS - 1)
    return jnp.where(dist < max_exact, dist, large)


def _shifted_bias(rel_bias, dist):
    rb = rel_bias.astype(F32)
    b = rb[_rel_bucket(dist)] - rb[NUM_BUCKETS - 1]
    b = jnp.where((dist >= 0)[..., None], b, NEG)
    return jnp.moveaxis(b, -1, 0)


def _split_halves(q):
    lane = lax.broadcasted_iota(jnp.int32, q.shape, 1)
    zero = jnp.zeros_like(q)
    return jnp.concatenate([jnp.where(lane < DK_ATT, q, zero), jnp.where(lane >= DK_ATT, q, zero)], axis=0)


def _sub_norm(o, w_row, lam_init):
    return o * lax.rsqrt(jnp.mean(o * o, axis=-1, keepdims=True) + EPS) * w_row * (1.0 - lam_init)


def _attn_prompt_kernel(lam_ref, q_ref, k_ref, v_ref, bd_ref, bs_ref, w_ref, o_ref, *, tq, lam_init):
    qi = pl.program_id(2)
    qs = _split_halves(q_ref[0])

    def block(j, bias, carry):
        m, l, acc = carry
        rows = pl.ds(pl.multiple_of(j * tq, tq), tq)
        s = lax.dot_general(qs, k_ref[0, rows, :], _NT, preferred_element_type=F32)
        if bias is not None:
            s = s + jnp.concatenate([bias, bias], axis=0)
        m_new = jnp.maximum(m, jnp.max(s, axis=-1, keepdims=True))
        alpha = jnp.exp(m - m_new)
        p = jnp.exp(s - m_new)
        l = l * alpha + jnp.sum(p, axis=-1, keepdims=True)
        acc = acc * alpha + jnp.dot(p.astype(BF16), v_ref[0, rows, :], preferred_element_type=F32)
        return m_new, l, acc

    carry = (jnp.full((2 * tq, 1), NEG, F32), jnp.zeros((2 * tq, 1), F32), jnp.zeros((2 * tq, DV_ATT), F32))
    carry = block(qi, bd_ref[0], carry)
    carry = block(jnp.maximum(qi - 1, 0), bs_ref[0] + jnp.where(qi == 0, NEG, 0.0), carry)
    carry = lax.fori_loop(0, jnp.maximum(qi - 1, 0), lambda j, cr: block(j, None, cr), carry)
    _, l, acc = carry
    o = acc[:tq] / l[:tq] - lam_ref[0] * (acc[tq:] / l[tq:])
    o_ref[0] = _sub_norm(o, w_ref[...], lam_init).astype(BF16)


def _attn_prompt(lam, qa, k16, v16, bias_diag, bias_sub, w_row, tq, lam_init):
    b, s, _ = qa.shape
    assert tq >= MAX_DISTANCE and s % tq == 0
    return pl.pallas_call(
        functools.partial(_attn_prompt_kernel, tq=tq, lam_init=lam_init),
        grid=(b, H_ATT, s // tq),
        in_specs=[pl.BlockSpec(memory_space=pltpu.SMEM),
                  pl.BlockSpec((1, tq, DV_ATT), lambda bi, h, qi: (bi, qi, h)),
                  pl.BlockSpec((1, s, DV_ATT), lambda bi, h, qi: (bi, 0, h)),
                  pl.BlockSpec((1, s, DV_ATT), lambda bi, h, qi: (bi, 0, h)),
                  pl.BlockSpec((1, tq, tq), lambda bi, h, qi: (h, 0, 0)),
                  pl.BlockSpec((1, tq, tq), lambda bi, h, qi: (h, 0, 0)),
                  _const_spec((1, DV_ATT))],
        out_specs=pl.BlockSpec((1, tq, DV_ATT), lambda bi, h, qi: (bi, qi, h)),
        out_shape=jax.ShapeDtypeStruct((b, s, W_ATT), BF16),
        compiler_params=pltpu.CompilerParams(dimension_semantics=("parallel", "parallel", "arbitrary")),
        name="attn_prompt",
    )(lam, qa, k16, v16, bias_diag, bias_sub, w_row)


def _attn_sample_kernel(pt_ref, lam_ref, wq_ref, kn_ref, vn_ref, bnew_ref, bfar_ref, btail_ref, w_ref, *rest,
                        t_new, lam_init):
    npg = PAGES_PER_STEP
    k_refs, v_refs = rest[:npg], rest[npg:2 * npg]
    o_ref, m_scr, l_scr, acc_scr = rest[2 * npg:]
    j = pl.program_id(1)
    wq = wq_ref[0]

    @pl.when(j == 0)
    def _():
        s = lax.dot_general(wq, kn_ref[0], _NT, preferred_element_type=F32) + bnew_ref[...]
        m = jnp.max(s, axis=-1, keepdims=True)
        p = jnp.exp(s - m)
        m_scr[...] = m
        l_scr[...] = jnp.sum(p, axis=-1, keepdims=True)
        acc_scr[...] = jnp.dot(p.astype(BF16), vn_ref[0], preferred_element_type=F32)

    s_all = []
    for i in range(npg):
        bias = btail_ref[0] if i == npg - 1 else bfar_ref[...]
        s_all.append(lax.dot_general(wq, k_refs[i][...].astype(BF16), _NT, preferred_element_type=F32) + bias)
    m_old = m_scr[...]
    m_new = m_old
    for s in s_all:
        m_new = jnp.maximum(m_new, jnp.max(s, axis=-1, keepdims=True))
    alpha = jnp.exp(m_old - m_new)
    l = l_scr[...] * alpha
    acc = acc_scr[...] * alpha
    for i in range(npg):
        p = jnp.exp(s_all[i] - m_new)
        l = l + jnp.sum(p, axis=-1, keepdims=True)
        acc = acc + jnp.dot(p.astype(BF16), v_refs[i][...].astype(BF16), preferred_element_type=F32)
    m_scr[...] = m_new
    l_scr[...] = l
    acc_scr[...] = acc

    @pl.when(j == pl.num_programs(1) - 1)
    def _():
        o = acc / l
        for h in range(H_ATT):
            r0 = h * 2 * t_new
            oh = o[r0:r0 + t_new] - lam_ref[0] * o[r0 + t_new:r0 + 2 * t_new]
            o_ref[0, :, h * DV_ATT:(h + 1) * DV_ATT] = _sub_norm(oh, w_ref[...], lam_init).astype(BF16)


def _attn_sample(page_table, lam, wq, k_new, v_new, bias_new, bias_far, bias_tail, w_row, cache_k, cache_v,
                 t_new, lam_init):
    nb, n_pages = page_table.shape
    npg = PAGES_PER_STEP
    assert n_pages % npg == 0
    n_steps = n_pages // npg
    r = wq.shape[1]
    prow = PAGE_SIZE * H_ATT

    def page_spec(i):
        return pl.BlockSpec((None, prow, DV_ATT), lambda b, j, pt: (pt[b, j * npg + i], 0, 0))

    per_b = lambda n: pl.BlockSpec((1, n, DV_ATT), lambda b, j, pt: (b, 0, 0))
    const = lambda shape: pl.BlockSpec(shape, lambda b, j, pt: (0,) * len(shape))
    grid_spec = pltpu.PrefetchScalarGridSpec(
        num_scalar_prefetch=1,
        grid=(nb, n_steps),
        in_specs=[pl.BlockSpec(memory_space=pltpu.SMEM), per_b(r), per_b(t_new * H_ATT), per_b(t_new * H_ATT),
                  const((r, t_new * H_ATT)), const((r, prow)),
                  pl.BlockSpec((1, r, prow), lambda b, j, pt: (jnp.where(j == n_steps - 1, 1, 0), 0, 0)),
                  const((1, DV_ATT))]
                 + [page_spec(i) for i in range(npg)] * 2,
        out_specs=pl.BlockSpec((1, t_new, W_ATT), lambda b, j, pt: (b, 0, 0)),
        scratch_shapes=[pltpu.VMEM((r, 1), F32), pltpu.VMEM((r, 1), F32), pltpu.VMEM((r, DV_ATT), F32)],
    )
    return pl.pallas_call(
        functools.partial(_attn_sample_kernel, t_new=t_new, lam_init=lam_init),
        grid_spec=grid_spec,
        out_shape=jax.ShapeDtypeStruct((nb, t_new, W_ATT), BF16),
        compiler_params=pltpu.CompilerParams(dimension_semantics=("parallel", "arbitrary"),
                                             vmem_limit_bytes=V7X_VMEM_LIMIT_BYTES),
        name="attn_sample",
    )(page_table, lam, wq, k_new, v_new, bias_new, bias_far, bias_tail, w_row,
      *([cache_k] * npg), *([cache_v] * npg))


def _out_mlp_kernel(x_ref, yr_ref, ya_ref, ga_ref, gb_ref, wa_ref, wb_ref, wo_ref, n2_ref, wu_ref, wd_ref,
                    y_ref, *, ff_chunk):
    a = jnp.dot(yr_ref[...], wa_ref[...], preferred_element_type=F32)
    b = jnp.dot(ya_ref[...], wb_ref[...], preferred_element_type=F32)
    mixed = ga_ref[...].astype(F32) * a + gb_ref[...].astype(F32) * b
    h = x_ref[...] + jnp.dot(mixed.astype(BF16), wo_ref[...], preferred_element_type=F32)
    ms = jnp.mean(h * h, axis=-1, keepdims=True)
    hn = (h * lax.rsqrt(ms + EPS) * n2_ref[...]).astype(BF16)
    y = h
    for c0 in range(0, D_FF, ff_chunk):
        u = jnp.maximum(jnp.dot(hn, wu_ref[:, c0:c0 + ff_chunk], preferred_element_type=F32), 0.0)
        y = y + jnp.dot((u * u).astype(BF16), wd_ref[c0:c0 + ff_chunk, :], preferred_element_type=F32)
    y_ref[...] = y


def _out_mlp(x2d, y_rec, y_att, ga, gb, wa16, wb16, wo16, n2, wu16, wd16, tm):
    m = x2d.shape[0]
    row = lambda w: pl.BlockSpec((tm, w), lambda i: (i, 0))
    return pl.pallas_call(
        functools.partial(_out_mlp_kernel, ff_chunk=D_MODEL),
        grid=(m // tm,),
        in_specs=[row(D_MODEL), row(W_REC), row(W_ATT), row(D_MODEL), row(D_MODEL),
                  _const_spec((W_REC, D_MODEL)), _const_spec((W_ATT, D_MODEL)), _const_spec((D_MODEL, D_MODEL)),
                  _const_spec((1, D_MODEL)), _const_spec((D_MODEL, D_FF)), _const_spec((D_FF, D_MODEL))],
        out_specs=row(D_MODEL),
        out_shape=jax.ShapeDtypeStruct((m, D_MODEL), F32),
        compiler_params=pltpu.CompilerParams(dimension_semantics=("parallel",),
                                             vmem_limit_bytes=V7X_VMEM_LIMIT_BYTES),
        name="out_mlp",
    )(x2d, y_rec, y_att, ga, gb, wa16, wb16, wo16, n2, wu16, wd16)


def kernel(x_prompt, x_sample, cache_k, cache_v, state_hgrn, page_table, norm1_w, w_in, lower_bounds, hgrn_norm_w, q_norm_w, k_norm_w, lam_q1, lam_k1, lam_q2, lam_k2, subln_w, rel_bias, w_branch_a, w_branch_b, w_out, norm2_w, w_up, w_down):
    depth = w_in.shape[0]
    assert depth == 1
    bp, sp, _ = x_prompt.shape
    bs, ts, _ = x_sample.shape
    n_pages = page_table.shape[1]
    past = n_pages * PAGE_SIZE
    assert PAGE_SIZE >= MAX_DISTANCE - 1 and ts <= HGRN_CHUNK
    l = 0
    lam_init = 0.8 - 0.6 * math.exp(-0.3 * l)
    scale = DK_ATT ** -0.5

    lb = jnp.cumsum(jax.nn.softmax(lower_bounds.astype(F32), axis=0), axis=0)[l][None, :]
    lam = (jnp.exp(jnp.dot(lam_q1[l].astype(F32), lam_k1[l].astype(F32)))
           - jnp.exp(jnp.dot(lam_q2[l].astype(F32), lam_k2[l].astype(F32))) + lam_init).reshape(1)
    n1 = norm1_w[l].astype(F32)[None, :]
    n2 = norm2_w[l].astype(F32)[None, :]
    qn_row = jnp.tile(q_norm_w[l].astype(F32).reshape(-1), H_ATT)[None, :] * scale
    kn_row = jnp.tile(k_norm_w[l].astype(F32).reshape(-1), H_ATT)[None, :]
    hw_row = hgrn_norm_w[l].astype(F32)[None, :]
    sw_row = subln_w[l].astype(F32)[None, :]
    seg = jnp.arange(W_ATT) // DK_ATT
    pseg = jnp.where(seg[:, None] == seg[None, :], 1.0 / DK_ATT, 0.0).astype(BF16)
    w_in16, wa16, wb16 = w_in[l].astype(BF16), w_branch_a[l].astype(BF16), w_branch_b[l].astype(BF16)
    wo16, wu16, wd16 = w_out[l].astype(BF16), w_up[l].astype(BF16), w_down[l].astype(BF16)

    def layer(x, s0, attend, tm, hgrn_bb, hgrn_tc):
        b, t, _ = x.shape
        x2d = x.reshape(b * t, D_MODEL)
        q, lf, vr, g, qa, k32, k16, v32, v16, ga, gb = _in_proj(x2d, n1, w_in16, lb, qn_row, kn_row, pseg, tm)
        r3 = lambda a: a.reshape(b, t, a.shape[-1])
        y_rec, s_new = _hgrn(r3(q), r3(lf), r3(vr), r3(g), s0, hw_row, hgrn_bb, hgrn_tc)
        y_att = attend(r3(qa), r3(k16), r3(v16))
        y = _out_mlp(x2d, y_rec.reshape(b * t, W_REC), y_att.reshape(b * t, W_ATT), ga, gb,
                     wa16, wb16, wo16, n2, wu16, wd16, tm)
        return (y.reshape(b, t, D_MODEL), k32.reshape(1, b, t, H_ATT, DV_ATT),
                v32.reshape(1, b, t, H_ATT, DV_ATT), s_new[None])

    tq = 256
    ri = jnp.arange(tq)
    bias_diag = _shifted_bias(rel_bias, ri[:, None] - ri[None, :])
    bias_sub = _shifted_bias(rel_bias, ri[:, None] + tq - ri[None, :])
    attend_p = lambda qa, k16, v16: _attn_prompt(lam, qa, k16, v16, bias_diag, bias_sub, sw_row, tq, lam_init)
    s0p = jnp.zeros((bp, H_REC, DK_REC, DV_REC), F32)
    y_p, k_p, v_p, s_p = layer(x_prompt, s0p, attend_p, 512, 1, 256)

    r = H_ATT * 2 * ts
    row_h = jnp.arange(r) // (2 * ts)
    row_t = jnp.arange(r) % ts
    col_h = jnp.arange(PAGE_SIZE * H_ATT) % H_ATT
    col_key = jnp.arange(PAGE_SIZE * H_ATT) // H_ATT

    def head_bias(dist, ch):
        bias = _shifted_bias(rel_bias, dist)
        own = jnp.take_along_axis(bias, jnp.broadcast_to(row_h[None, :, None], (1,) + dist.shape), axis=0)[0]
        return jnp.where(row_h[:, None] == ch[None, :], own, NEG)

    zero_dist = jnp.full((r, PAGE_SIZE * H_ATT), MAX_DISTANCE, jnp.int32)
    bias_far = head_bias(zero_dist, col_h)
    tail_dist = (past + row_t)[:, None] - ((n_pages - 1) * PAGE_SIZE + col_key)[None, :]
    bias_tail = jnp.stack([bias_far, head_bias(tail_dist, col_h)])
    new_h = jnp.arange(ts * H_ATT) % H_ATT
    new_t = jnp.arange(ts * H_ATT) // H_ATT
    bias_new = head_bias(row_t[:, None] - new_t[None, :], new_h)
    ck = cache_k[l].reshape(cache_k.shape[1], PAGE_SIZE * H_ATT, DV_ATT)
    cv = cache_v[l].reshape(cache_v.shape[1], PAGE_SIZE * H_ATT, DV_ATT)

    def attend_s(qa, k16, v16):
        q4 = qa.reshape(bs, ts, H_ATT, DV_ATT).transpose(0, 2, 1, 3)
        lane = jnp.arange(DV_ATT)
        wq = jnp.stack([jnp.where(lane < DK_ATT, q4, 0), jnp.where(lane >= DK_ATT, q4, 0)], axis=2)
        wq = wq.reshape(bs, r, DV_ATT).astype(BF16)
        return _attn_sample(page_table, lam, wq, k16.reshape(bs, ts * H_ATT, DV_ATT),
                            v16.reshape(bs, ts * H_ATT, DV_ATT), bias_new, bias_far, bias_tail, sw_row,
                            ck, cv, ts, lam_init)

    y_s, k_s, v_s, s_s = layer(x_sample, state_hgrn[l].astype(F32), attend_s, 512, 8, ts)
    return (y_p, y_s, k_p, v_p, s_p, k_s, v_s, s_s)
```

```python
import functools
import math

import jax
import jax.numpy as jnp
from jax import lax
from jax.experimental import pallas as pl
from jax.experimental.pallas import tpu as pltpu

F32 = jnp.float32
BF16 = jnp.bfloat16

D_MODEL = 1024
H_REC, DK_REC, DV_REC = 4, 128, 128
W_REC = H_REC * DK_REC
H_ATT, DK_ATT = 4, 64
DV_ATT = 2 * DK_ATT
W_ATT = H_ATT * DV_ATT
D_FF = 4 * D_MODEL
NUM_BUCKETS = 32
MAX_DISTANCE = 128
PAGE_SIZE = 128
HGRN_CHUNK = 32
EPS = 1e-6
NEG = -1e30
LOG2E = math.log2(math.e)

OFF_HQ, OFF_HF, OFF_HI, OFF_HG = 0, W_REC, 2 * W_REC, 3 * W_REC
OFF_AQ = 4 * W_REC
OFF_AK = OFF_AQ + W_ATT
OFF_AV = OFF_AK + W_ATT
OFF_GA = OFF_AV + W_ATT
OFF_GB = OFF_GA + D_MODEL
IN_WIDTH = OFF_GB + D_MODEL

V7X_VMEM_LIMIT_BYTES = 56 * 1024 * 1024
V7X_LANES = 128
PAGES_PER_STEP = 8
PAGE_GROUP = 4
ATT_BLOCK = 256

_NT = (((1,), (1,)), ((), ()))


def _const_spec(shape):
    n = len(shape)
    return pl.BlockSpec(shape, lambda *_: (0,) * n, pipeline_mode=pl.Buffered(1))


def _sigmoid(x):
    return 1.0 / (1.0 + jnp.exp(-x))


def _in_proj_kernel(x_ref, n1_ref, w_ref, lb_ref, qn_ref, kn_ref, pseg_ref,
                    q_ref, lf_ref, vr_ref, g_ref, qa_ref, k32_ref, k16_ref, v32_ref, vt_ref,
                    ga_ref, gb_ref):
    tm = x_ref.shape[0]
    x = x_ref[...]
    ms = jnp.mean(x * x, axis=-1, keepdims=True)
    xb = (x * lax.rsqrt(ms + EPS) * n1_ref[...]).astype(BF16)

    def proj(off, size):
        return jnp.dot(xb, w_ref[:, off:off + size], preferred_element_type=F32)

    def seg_rms(z):
        z2 = z * z
        hi = z2.astype(BF16)
        lo = (z2 - hi.astype(F32)).astype(BF16)
        p = pseg_ref[...]
        m = jnp.dot(hi, p, preferred_element_type=F32) + jnp.dot(lo, p, preferred_element_type=F32)
        return z * lax.rsqrt(m + EPS)

    def store_head_rows(ref, z):
        for h in range(H_ATT):
            ref[pl.ds(h, tm, stride=H_ATT), :] = z[:, h * DV_ATT:(h + 1) * DV_ATT]

    hq = proj(OFF_HQ, W_REC)
    q_ref[...] = (hq * _sigmoid(hq)).astype(BF16)
    hf = proj(OFF_HF, W_REC)
    lb = lb_ref[...]
    lf_ref[...] = jnp.log(lb + (1.0 - lb) * _sigmoid(hf))
    vr_ref[...] = proj(OFF_HI, W_REC).astype(BF16)
    hg = proj(OFF_HG, W_REC)
    g_ref[...] = (hg * _sigmoid(hg)).astype(BF16)

    qa_ref[...] = (seg_rms(proj(OFF_AQ, W_ATT)) * qn_ref[...]).astype(BF16)
    ka = seg_rms(proj(OFF_AK, W_ATT)) * kn_ref[...]
    store_head_rows(k32_ref, ka)
    k16_ref[...] = ka.astype(BF16)
    va = proj(OFF_AV, W_ATT)
    store_head_rows(v32_ref, va)
    if len(vt_ref.shape) == 2:
        vt_ref[...] = va.astype(BF16)
    else:
        kb = vt_ref.shape[2]
        for i in range(vt_ref.shape[0]):
            vt_ref[i] = va[i * kb:(i + 1) * kb, :].T.astype(BF16)
    ga_ref[...] = _sigmoid(proj(OFF_GA, D_MODEL)).astype(BF16)
    gb_ref[...] = _sigmoid(proj(OFF_GB, D_MODEL)).astype(BF16)


def _in_proj(x2d, n1, w_in16, lb, qn_row, kn_row, pseg, tm, kb):
    m = x2d.shape[0]
    row = lambda w: pl.BlockSpec((tm, w), lambda i: (i, 0))
    head_rows = pl.BlockSpec((tm * H_ATT, DV_ATT), lambda i: (i, 0))
    if kb is None:
        vt_spec, vt_shape = row(W_ATT), (m, W_ATT)
    else:
        vt_spec, vt_shape = pl.BlockSpec((tm // kb, W_ATT, kb), lambda i: (i, 0, 0)), (m // kb, W_ATT, kb)
    sds = jax.ShapeDtypeStruct
    return pl.pallas_call(
        _in_proj_kernel,
        grid=(m // tm,),
        in_specs=[row(D_MODEL), _const_spec((1, D_MODEL)), _const_spec((D_MODEL, IN_WIDTH)),
                  _const_spec((1, W_REC)), _const_spec((1, W_ATT)), _const_spec((1, W_ATT)),
                  _const_spec((W_ATT, W_ATT))],
        out_specs=[row(W_REC), row(W_REC), row(W_REC), row(W_REC), row(W_ATT), head_rows, row(W_ATT),
                   head_rows, vt_spec, row(D_MODEL), row(D_MODEL)],
        out_shape=[sds((m, W_REC), BF16), sds((m, W_REC), F32), sds((m, W_REC), BF16), sds((m, W_REC), BF16),
                   sds((m, W_ATT), BF16), sds((m * H_ATT, DV_ATT), F32), sds((m, W_ATT), BF16),
                   sds((m * H_ATT, DV_ATT), F32), sds(vt_shape, BF16),
                   sds((m, D_MODEL), BF16), sds((m, D_MODEL), BF16)],
        compiler_params=pltpu.CompilerParams(dimension_semantics=("parallel",),
                                             vmem_limit_bytes=V7X_VMEM_LIMIT_BYTES),
        name="in_proj",
    )(x2d, n1, w_in16, lb, qn_row, kn_row, pseg)


def _hgrn_kernel(q_ref, lf_ref, v_ref, g_ref, s0_ref, w_ref, y_ref, s_ref, st_scr, *, c, carry):
    r = q_ref.shape[1]
    nc = r // c
    t = pl.program_id(1)
    mm = BF16 if c >= 16 else F32
    shift = c.bit_length() - 1
    r_i = lax.broadcasted_iota(jnp.int32, (r, r), 0)
    c_i = lax.broadcasted_iota(jnp.int32, (r, r), 1)
    tril = ((r_i >> shift) == (c_i >> shift)) & (r_i >= c_i)
    tril16 = jnp.where(tril, 1.0, 0.0).astype(BF16)
    lane_chunk = lax.broadcasted_iota(jnp.int32, (DV_REC, r), 1) >> shift

    if carry:
        @pl.when(t == 0)
        def _():
            for h in range(H_REC):
                st_scr[h] = s0_ref[0, h].T

    for h in range(H_REC):
        cols = slice(h * DK_REC, (h + 1) * DK_REC)
        lf = lf_ref[0, :, cols]
        hi = lf.astype(BF16)
        lo = (lf - hi.astype(F32)).astype(BF16)
        b = jnp.dot(tril16, hi, preferred_element_type=F32) + jnp.dot(tril16, lo, preferred_element_type=F32)
        b3 = b.reshape(nc, c, DK_REC)
        bl = b3[:, c - 1:c, :]
        k = 1.0 - jnp.exp(lf)
        q = q_ref[0, :, cols].astype(F32)
        v16 = v_ref[0, :, cols]
        qe = q * jnp.exp(b)
        qe16 = qe.astype(BF16)
        ke16 = (k * jnp.exp(-b)).astype(BF16)
        kd16 = (k.reshape(nc, c, DK_REC) * jnp.exp(bl - b3)).reshape(r, DK_REC).astype(BF16)
        dec = jnp.exp(bl)
        a = lax.dot_general(qe16, ke16, _NT, preferred_element_type=F32)
        o = jnp.dot(jnp.where(tril, a, 0.0).astype(BF16), v16, preferred_element_type=F32)
        vt = v16.astype(F32).T
        lhs = jnp.concatenate([jnp.where(lane_chunk == n, vt, 0.0) for n in range(nc)], axis=0).astype(BF16)
        u = jnp.dot(lhs, kd16, preferred_element_type=F32)
        qe_mm = qe16 if mm == BF16 else qe
        parts = []
        if carry:
            st = st_scr[h]
            for n in range(nc):
                parts.append(lax.dot_general(qe_mm[n * c:(n + 1) * c], st.astype(mm), _NT,
                                             preferred_element_type=F32))
                st = st * dec[n] + u[n * DV_REC:(n + 1) * DV_REC]
            st_scr[h] = st
        else:
            for n in range(nc):
                st = s0_ref[n, h].T
                parts.append(lax.dot_general(qe_mm[n * c:(n + 1) * c], st.astype(mm), _NT,
                                             preferred_element_type=F32))
                s_ref[n, h] = (st * dec[n] + u[n * DV_REC:(n + 1) * DV_REC]).T
        o = o + jnp.concatenate(parts, axis=0)
        on = o * lax.rsqrt(jnp.mean(o * o, axis=-1, keepdims=True) + EPS) * w_ref[...]
        y_ref[0, :, cols] = (on * g_ref[0, :, cols].astype(F32)).astype(BF16)

    if carry:
        @pl.when(t == pl.num_programs(1) - 1)
        def _():
            for h in range(H_REC):
                s_ref[0, h] = st_scr[h].T


def _hgrn(q, lf, v, g, s0, w_row, rows, c, carry):
    gdim, t, _ = q.shape
    nseq = 1 if carry else rows // c
    blk = pl.BlockSpec((1, rows, W_REC), lambda gi, ti: (gi, ti, 0))
    if carry:
        sblk = pl.BlockSpec((1, H_REC, DK_REC, DV_REC), lambda gi, ti: (gi, 0, 0, 0))
    else:
        sblk = pl.BlockSpec((nseq, H_REC, DK_REC, DV_REC), lambda gi, ti: (ti, 0, 0, 0))
    return pl.pallas_call(
        functools.partial(_hgrn_kernel, c=c, carry=carry),
        grid=(gdim, t // rows),
        in_specs=[blk, blk, blk, blk, sblk, _const_spec((1, DV_REC))],
        out_specs=[blk, sblk],
        out_shape=[jax.ShapeDtypeStruct((gdim, t, W_REC), BF16), jax.ShapeDtypeStruct(s0.shape, F32)],
        scratch_shapes=[pltpu.VMEM((H_REC, DV_REC, DK_REC), F32)],
        compiler_params=pltpu.CompilerParams(dimension_semantics=("parallel", "arbitrary"),
                                             vmem_limit_bytes=V7X_VMEM_LIMIT_BYTES),
        name="hgrn",
    )(q, lf, v, g, s0, w_row)


def _rel_bucket(dist):
    dist = jnp.maximum(dist, 0)
    max_exact = NUM_BUCKETS // 2
    large = max_exact + (jnp.log(jnp.maximum(dist, 1).astype(F32) / max_exact)
                         / math.log(MAX_DISTANCE / max_exact) * (NUM_BUCKETS - max_exact)).astype(jnp.int32)
    large = jnp.minimum(large, NUM_BUCKETS - 1)
    return jnp.where(dist < max_exact, dist, large)


def _shifted_bias(rel_bias, dist):
    rb = rel_bias.astype(F32)
    rb = (rb - rb[NUM_BUCKETS - 1]) * LOG2E
    bucket = _rel_bucket(dist)[None]
    expand = (slice(None),) + (None,) * dist.ndim
    out = jnp.zeros((rb.shape[1],) + dist.shape, F32)
    for kb in range(NUM_BUCKETS - 1):
        out = jnp.where(bucket == kb, rb[kb][expand], out)
    return jnp.where((dist >= 0)[None], out, NEG)


def _split_halves(q):
    lane = lax.broadcasted_iota(jnp.int32, q.shape, 1)
    zero = jnp.zeros_like(q)
    return jnp.concatenate([jnp.where(lane < DK_ATT, q, zero), jnp.where(lane >= DK_ATT, q, zero)], axis=0)


def _attn_prompt_kernel(lam_ref, q_ref, k_ref, vt_ref, bd_ref, bs_ref, w_ref, o_ref, *, tq, lam_init):
    qi = pl.program_id(2)
    qs = _split_halves(q_ref[0])
    nt = 2 * tq // V7X_LANES
    q_tiles = [qs[i * V7X_LANES:(i + 1) * V7X_LANES] for i in range(nt)]

    def scores(j):
        k_blk = k_ref[0, pl.ds(pl.multiple_of(j * tq, tq), tq), :]
        return tuple(lax.dot_general(k_blk, q_tiles[i], _NT, preferred_element_type=F32) for i in range(nt))

    def update(j, s_tiles, bias_ref, extra, carry):
        vt_blk = vt_ref[j]
        out = []
        for i in range(nt):
            m, l, acc = carry[i]
            s = s_tiles[i]
            if bias_ref is not None:
                c0 = (i * V7X_LANES) % tq
                s = s + bias_ref[0, :, c0:c0 + V7X_LANES]
                if extra is not None:
                    s = s + extra
            m_new = jnp.maximum(m, jnp.max(s, axis=0, keepdims=True))
            alpha = jnp.exp2(m - m_new)
            p = jnp.exp2(s - m_new)
            l = l * alpha + jnp.sum(p, axis=0, keepdims=True)
            acc = acc * alpha + jnp.dot(vt_blk, p.astype(BF16), preferred_element_type=F32)
            out.append((m_new, l, acc))
        return tuple(out)

    init = tuple((jnp.full((1, V7X_LANES), NEG, F32), jnp.zeros((1, V7X_LANES), F32),
                  jnp.zeros((DV_ATT, V7X_LANES), F32)) for _ in range(nt))
    sub_j = jnp.maximum(qi - 1, 0)
    n_far = jnp.maximum(qi - 1, 0)
    s_diag = scores(qi)
    s_sub = scores(sub_j)
    carry = update(qi, s_diag, bd_ref, None, init)
    s_far = scores(0)
    carry = update(sub_j, s_sub, bs_ref, jnp.where(qi == 0, NEG, 0.0), carry)

    def far_step(j, state):
        s_cur, cr = state
        s_next = scores(jnp.minimum(j + 1, qi))
        return s_next, update(j, s_cur, None, None, cr)

    _, carry = lax.fori_loop(0, n_far, far_step, (s_far, carry))
    o_t = jnp.concatenate([acc / l for _, l, acc in carry], axis=1)
    o_t = o_t[:, :tq] - lam_ref[0] * o_t[:, tq:]
    o_t = o_t * lax.rsqrt(jnp.mean(o_t * o_t, axis=0, keepdims=True) + EPS)
    o_ref[0] = (o_t.T * w_ref[...] * (1.0 - lam_init)).astype(BF16)


def _attn_prompt(lam, qa, k16, vt, bias_diag, bias_sub, w_row, tq, lam_init):
    b, s, _ = qa.shape
    nb = s // tq
    assert tq >= MAX_DISTANCE and s % tq == 0
    return pl.pallas_call(
        functools.partial(_attn_prompt_kernel, tq=tq, lam_init=lam_init),
        grid=(b, H_ATT, nb),
        in_specs=[pl.BlockSpec(memory_space=pltpu.SMEM),
                  pl.BlockSpec((1, tq, DV_ATT), lambda bi, h, qi: (bi, qi, h)),
                  pl.BlockSpec((1, s, DV_ATT), lambda bi, h, qi: (bi, 0, h)),
                  pl.BlockSpec((nb, DV_ATT, tq), lambda bi, h, qi: (bi, h, 0)),
                  pl.BlockSpec((1, tq, tq), lambda bi, h, qi: (h, 0, 0)),
                  pl.BlockSpec((1, tq, tq), lambda bi, h, qi: (h, 0, 0)),
                  _const_spec((1, DV_ATT))],
        out_specs=pl.BlockSpec((1, tq, DV_ATT), lambda bi, h, qi: (bi, qi, h)),
        out_shape=jax.ShapeDtypeStruct((b, s, W_ATT), BF16),
        compiler_params=pltpu.CompilerParams(dimension_semantics=("parallel", "parallel", "arbitrary")),
        name="attn_prompt",
    )(lam, qa, k16, vt, bias_diag, bias_sub, w_row)


def _attn_sample_kernel(pt_ref, lam_ref, wq_ref, kn_ref, vn_ref, bnew_ref, btail_ref, w_ref, *rest,
                        t_new, lam_init):
    npg = PAGES_PER_STEP
    k_refs, v_refs = rest[:npg], rest[npg:2 * npg]
    o_ref, m_scr, l_scr, acc_scr = rest[2 * npg:]
    j = pl.program_id(1)
    rh = 2 * t_new
    wq = wq_ref[0]
    wq_h = [wq[h * rh:(h + 1) * rh] for h in range(H_ATT)]

    def head_rows(ref, h):
        return ref[pl.ds(h, PAGE_SIZE, stride=H_ATT), :].astype(BF16)

    @pl.when(j == 0)
    def _():
        s = jnp.concatenate([lax.dot_general(wq_h[h], kn_ref[0, :, h * DV_ATT:(h + 1) * DV_ATT], _NT,
                                             preferred_element_type=F32) for h in range(H_ATT)], axis=0)
        s = s + bnew_ref[...]
        m = jnp.max(s, axis=-1, keepdims=True)
        p = jnp.exp2(s - m)
        m_scr[...] = m
        l_scr[...] = jnp.sum(p, axis=-1, keepdims=True)
        acc_scr[...] = jnp.concatenate(
            [jnp.dot(p[h * rh:(h + 1) * rh].astype(BF16), vn_ref[0, :, h * DV_ATT:(h + 1) * DV_ATT],
                     preferred_element_type=F32) for h in range(H_ATT)], axis=0)

    s_pages = []
    for i in range(npg):
        s = jnp.concatenate([lax.dot_general(wq_h[h], head_rows(k_refs[i], h), _NT, preferred_element_type=F32)
                             for h in range(H_ATT)], axis=0)
        if i == npg - 1:
            s = s + btail_ref[0]
        s_pages.append(s)
    m_run, l, acc = m_scr[...], l_scr[...], acc_scr[...]
    for g0 in range(0, npg, PAGE_GROUP):
        group = range(g0, g0 + PAGE_GROUP)
        mx = s_pages[g0]
        for i in group[1:]:
            mx = jnp.maximum(mx, s_pages[i])
        m_new = jnp.maximum(m_run, jnp.max(mx, axis=-1, keepdims=True))
        alpha = jnp.exp2(m_run - m_new)
        acc = acc * alpha
        psum = None
        for i in group:
            p = jnp.exp2(s_pages[i] - m_new)
            psum = p if psum is None else psum + p
            acc = acc + jnp.concatenate(
                [jnp.dot(p[h * rh:(h + 1) * rh].astype(BF16), head_rows(v_refs[i], h), preferred_element_type=F32)
                 for h in range(H_ATT)], axis=0)
        l = l * alpha + jnp.sum(psum, axis=-1, keepdims=True)
        m_run = m_new
    m_scr[...] = m_run
    l_scr[...] = l
    acc_scr[...] = acc

    @pl.when(j == pl.num_programs(1) - 1)
    def _():
        o = acc / l
        for h in range(H_ATT):
            oh = o[h * rh:h * rh + t_new] - lam_ref[0] * o[h * rh + t_new:(h + 1) * rh]
            oh = oh * lax.rsqrt(jnp.mean(oh * oh, axis=-1, keepdims=True) + EPS) * w_ref[...] * (1.0 - lam_init)
            o_ref[0, :, h * DV_ATT:(h + 1) * DV_ATT] = oh.astype(BF16)


def _attn_sample(page_table, lam, wq, k_new, v_new, bias_new, bias_tail, w_row, cache_k, cache_v,
                 t_new, lam_init):
    nb, n_pages = page_table.shape
    npg = PAGES_PER_STEP
    assert n_pages % npg == 0
    n_steps = n_pages // npg
    r = wq.shape[1]
    prow = PAGE_SIZE * H_ATT

    def page_spec(i):
        return pl.BlockSpec((None, prow, DV_ATT), lambda b, j, pt: (pt[b, j * npg + i], 0, 0))

    per_b = lambda n, w: pl.BlockSpec((1, n, w), lambda b, j, pt: (b, 0, 0))
    const = lambda shape: pl.BlockSpec(shape, lambda b, j, pt: (0,) * len(shape))
    grid_spec = pltpu.PrefetchScalarGridSpec(
        num_scalar_prefetch=1,
        grid=(nb, n_steps),
        in_specs=[pl.BlockSpec(memory_space=pltpu.SMEM), per_b(r, DV_ATT), per_b(t_new, W_ATT),
                  per_b(t_new, W_ATT), const((r, t_new)),
                  pl.BlockSpec((1, r, PAGE_SIZE), lambda b, j, pt: (jnp.where(j == n_steps - 1, 1, 0), 0, 0)),
                  const((1, DV_ATT))]
                 + [page_spec(i) for i in range(npg)] * 2,
        out_specs=pl.BlockSpec((1, t_new, W_ATT), lambda b, j, pt: (b, 0, 0)),
        scratch_shapes=[pltpu.VMEM((r, 1), F32), pltpu.VMEM((r, 1), F32), pltpu.VMEM((r, DV_ATT), F32)],
    )
    return pl.pallas_call(
        functools.partial(_attn_sample_kernel, t_new=t_new, lam_init=lam_init),
        grid_spec=grid_spec,
        out_shape=jax.ShapeDtypeStruct((nb, t_new, W_ATT), BF16),
        compiler_params=pltpu.CompilerParams(dimension_semantics=("parallel", "arbitrary"),
                                             vmem_limit_bytes=V7X_VMEM_LIMIT_BYTES),
        name="attn_sample",
    )(page_table, lam, wq, k_new, v_new, bias_new, bias_tail, w_row,
      *([cache_k] * npg), *([cache_v] * npg))


def _out_mlp_kernel(x_ref, yr_ref, ya_ref, ga_ref, gb_ref, wa_ref, wb_ref, wo_ref, n2_ref, wu_ref, wd_ref,
                    y_ref, *, ff_chunk):
    a = jnp.dot(yr_ref[...], wa_ref[...], preferred_element_type=F32)
    b = jnp.dot(ya_ref[...], wb_ref[...], preferred_element_type=F32)
    mixed = ga_ref[...].astype(F32) * a + gb_ref[...].astype(F32) * b
    h = x_ref[...] + jnp.dot(mixed.astype(BF16), wo_ref[...], preferred_element_type=F32)
    ms = jnp.mean(h * h, axis=-1, keepdims=True)
    hn = (h * lax.rsqrt(ms + EPS) * n2_ref[...]).astype(BF16)
    y = h
    for c0 in range(0, D_FF, ff_chunk):
        u = jnp.maximum(jnp.dot(hn, wu_ref[:, c0:c0 + ff_chunk], preferred_element_type=F32), 0.0)
        y = y + jnp.dot((u * u).astype(BF16), wd_ref[c0:c0 + ff_chunk, :], preferred_element_type=F32)
    y_ref[...] = y


def _out_mlp(x2d, y_rec, y_att, ga, gb, wa16, wb16, wo16, n2, wu16, wd16, tm):
    m = x2d.shape[0]
    row = lambda w: pl.BlockSpec((tm, w), lambda i: (i, 0))
    return pl.pallas_call(
        functools.partial(_out_mlp_kernel, ff_chunk=D_MODEL),
        grid=(m // tm,),
        in_specs=[row(D_MODEL), row(W_REC), row(W_ATT), row(D_MODEL), row(D_MODEL),
                  _const_spec((W_REC, D_MODEL)), _const_spec((W_ATT, D_MODEL)), _const_spec((D_MODEL, D_MODEL)),
                  _const_spec((1, D_MODEL)), _const_spec((D_MODEL, D_FF)), _const_spec((D_FF, D_MODEL))],
        out_specs=row(D_MODEL),
        out_shape=jax.ShapeDtypeStruct((m, D_MODEL), F32),
        compiler_params=pltpu.CompilerParams(dimension_semantics=("parallel",),
                                             vmem_limit_bytes=V7X_VMEM_LIMIT_BYTES),
        name="out_mlp",
    )(x2d, y_rec, y_att, ga, gb, wa16, wb16, wo16, n2, wu16, wd16)


def kernel(x_prompt, x_sample, cache_k, cache_v, state_hgrn, page_table, norm1_w, w_in, lower_bounds, hgrn_norm_w, q_norm_w, k_norm_w, lam_q1, lam_k1, lam_q2, lam_k2, subln_w, rel_bias, w_branch_a, w_branch_b, w_out, norm2_w, w_up, w_down):
    depth = w_in.shape[0]
    assert depth == 1
    bp, sp, _ = x_prompt.shape
    bs, ts, _ = x_sample.shape
    n_pages = page_table.shape[1]
    past = n_pages * PAGE_SIZE
    assert PAGE_SIZE >= MAX_DISTANCE - 1 and ts <= HGRN_CHUNK
    l = 0
    lam_init = 0.8 - 0.6 * math.exp(-0.3 * l)
    scale = DK_ATT ** -0.5

    lb = jnp.cumsum(jax.nn.softmax(lower_bounds.astype(F32), axis=0), axis=0)[l][None, :]
    lam = (jnp.exp(jnp.dot(lam_q1[l].astype(F32), lam_k1[l].astype(F32)))
           - jnp.exp(jnp.dot(lam_q2[l].astype(F32), lam_k2[l].astype(F32))) + lam_init).reshape(1)
    n1 = norm1_w[l].astype(F32)[None, :]
    n2 = norm2_w[l].astype(F32)[None, :]
    qn_row = jnp.tile(q_norm_w[l].astype(F32).reshape(-1), H_ATT)[None, :] * (scale * LOG2E)
    kn_row = jnp.tile(k_norm_w[l].astype(F32).reshape(-1), H_ATT)[None, :]
    hw_row = hgrn_norm_w[l].astype(F32)[None, :]
    sw_row = subln_w[l].astype(F32)[None, :]
    seg = jnp.arange(W_ATT) // DK_ATT
    pseg = jnp.where(seg[:, None] == seg[None, :], 1.0 / DK_ATT, 0.0).astype(BF16)
    w_in16, wa16, wb16 = w_in[l].astype(BF16), w_branch_a[l].astype(BF16), w_branch_b[l].astype(BF16)
    wo16, wu16, wd16 = w_out[l].astype(BF16), w_up[l].astype(BF16), w_down[l].astype(BF16)

    def layer(x, attend, hgrn_fn, tm, kb):
        b, t, _ = x.shape
        x2d = x.reshape(b * t, D_MODEL)
        q, lf, vr, g, qa, k32, k16, v32, vt, ga, gb = _in_proj(x2d, n1, w_in16, lb, qn_row, kn_row, pseg, tm, kb)
        y_rec, s_new = hgrn_fn(q, lf, vr, g)
        y_att = attend(qa, k16, vt)
        y = _out_mlp(x2d, y_rec.reshape(b * t, W_REC), y_att.reshape(b * t, W_ATT), ga, gb,
                     wa16, wb16, wo16, n2, wu16, wd16, tm)
        return (y.reshape(b, t, D_MODEL), k32.reshape(1, b, t, H_ATT, DV_ATT),
                v32.reshape(1, b, t, H_ATT, DV_ATT), s_new[None])

    tq = ATT_BLOCK
    ri = jnp.arange(tq)
    bias_diag = _shifted_bias(rel_bias, ri[None, :] - ri[:, None])
    bias_sub = _shifted_bias(rel_bias, ri[None, :] + tq - ri[:, None])
    r3p = lambda a: a.reshape(bp, sp, a.shape[-1])
    attend_p = lambda qa, k16, vt: _attn_prompt(lam, r3p(qa), r3p(k16), vt, bias_diag, bias_sub, sw_row, tq, lam_init)
    s0p = jnp.zeros((bp, H_REC, DK_REC, DV_REC), F32)
    hgrn_p = lambda q, lf, vr, g: _hgrn(r3p(q), r3p(lf), r3p(vr), r3p(g), s0p, hw_row, 256, HGRN_CHUNK, True)
    y_p, k_p, v_p, s_p = layer(x_prompt, attend_p, hgrn_p, 512, tq)

    r = H_ATT * 2 * ts
    row_h = jnp.arange(r) // (2 * ts)
    row_t = jnp.arange(r) % ts

    def own_head(bias):
        out = jnp.zeros(bias.shape[1:], F32)
        for h in range(H_ATT):
            out = jnp.where((row_h == h)[:, None], bias[h], out)
        return out

    tail_dist = (past + row_t)[:, None] - ((n_pages - 1) * PAGE_SIZE + jnp.arange(PAGE_SIZE))[None, :]
    bias_tail = jnp.stack([jnp.zeros((r, PAGE_SIZE), F32), own_head(_shifted_bias(rel_bias, tail_dist))])
    bias_new = own_head(_shifted_bias(rel_bias, row_t[:, None] - jnp.arange(ts)[None, :]))
    ck = cache_k[l].reshape(cache_k.shape[1], PAGE_SIZE * H_ATT, DV_ATT)
    cv = cache_v[l].reshape(cache_v.shape[1], PAGE_SIZE * H_ATT, DV_ATT)

    def attend_s(qa, k16, vt):
        q4 = qa.reshape(bs, ts, H_ATT, DV_ATT).transpose(0, 2, 1, 3)
        lane = jnp.arange(DV_ATT)
        wq = jnp.stack([jnp.where(lane < DK_ATT, q4, 0), jnp.where(lane >= DK_ATT, q4, 0)], axis=2)
        wq = wq.reshape(bs, r, DV_ATT).astype(BF16)
        return _attn_sample(page_table, lam, wq, k16.reshape(bs, ts, W_ATT), vt.reshape(bs, ts, W_ATT),
                            bias_new, bias_tail, sw_row, ck, cv, ts, lam_init)

    r3s = lambda a: a.reshape(1, bs * ts, a.shape[-1])
    hgrn_rows = 16 * ts
    hgrn_s = lambda q, lf, vr, g: _hgrn(r3s(q), r3s(lf), r3s(vr), r3s(g), state_hgrn[l].astype(F32), hw_row,
                                        hgrn_rows, ts, False)
    y_s, k_s, v_s, s_s = layer(x_sample, attend_s, hgrn_s, 512, None)
    return (y_p, y_s, k_p, v_p, s_p, k_s, v_s, s_s)
```

```python
import functools
import math

import jax
import jax.numpy as jnp
from jax import lax
from jax.experimental import pallas as pl
from jax.experimental.pallas import tpu as pltpu

F32 = jnp.float32
BF16 = jnp.bfloat16

D_MODEL = 1024
H_REC, DK_REC, DV_REC = 4, 128, 128
W_REC = H_REC * DK_REC
H_ATT, DK_ATT = 4, 64
DV_ATT = 2 * DK_ATT
W_ATT = H_ATT * DV_ATT
D_FF = 4 * D_MODEL
NUM_BUCKETS = 32
MAX_DISTANCE = 128
PAGE_SIZE = 128
HGRN_CHUNK = 32
EPS = 1e-6
NEG = -1e30
LOG2E = math.log2(math.e)

OFF_HQ, OFF_HF, OFF_HI, OFF_HG = 0, W_REC, 2 * W_REC, 3 * W_REC
OFF_AQ = 4 * W_REC
OFF_AK = OFF_AQ + W_ATT
OFF_AV = OFF_AK + W_ATT
OFF_GA = OFF_AV + W_ATT
OFF_GB = OFF_GA + D_MODEL
IN_WIDTH = OFF_GB + D_MODEL

V7X_VMEM_LIMIT_BYTES = 56 * 1024 * 1024
V7X_LANES = 128
PAGES_PER_STEP = 16
PAGE_BUFFERS = 2
ATT_BLOCK = 256

_NT = (((1,), (1,)), ((), ()))


def _const_spec(shape):
    n = len(shape)
    return pl.BlockSpec(shape, lambda *_: (0,) * n, pipeline_mode=pl.Buffered(1))


def _sigmoid(x):
    return 1.0 / (1.0 + jnp.exp(-x))


def _in_proj_kernel(x_ref, n1_ref, w_ref, lb_ref, qn_ref, kn_ref, pseg_ref,
                    q_ref, lf_ref, vr_ref, g_ref, qa_ref, k32_ref, k16_ref, v32_ref, vt_ref,
                    ga_ref, gb_ref):
    tm = x_ref.shape[0]
    x = x_ref[...]
    ms = jnp.mean(x * x, axis=-1, keepdims=True)
    xb = (x * lax.rsqrt(ms + EPS) * n1_ref[...]).astype(BF16)

    def proj(off, size):
        return jnp.dot(xb, w_ref[:, off:off + size], preferred_element_type=F32)

    def seg_rms(z):
        z2 = z * z
        hi = z2.astype(BF16)
        lo = (z2 - hi.astype(F32)).astype(BF16)
        p = pseg_ref[...]
        m = jnp.dot(hi, p, preferred_element_type=F32) + jnp.dot(lo, p, preferred_element_type=F32)
        return z * lax.rsqrt(m + EPS)

    def store_head_rows(ref, z):
        for h in range(H_ATT):
            ref[pl.ds(h, tm, stride=H_ATT), :] = z[:, h * DV_ATT:(h + 1) * DV_ATT]

    hq = proj(OFF_HQ, W_REC)
    q_ref[...] = (hq * _sigmoid(hq)).astype(BF16)
    hf = proj(OFF_HF, W_REC)
    lb = lb_ref[...]
    lf_ref[...] = jnp.log(lb + (1.0 - lb) * _sigmoid(hf))
    vr_ref[...] = proj(OFF_HI, W_REC).astype(BF16)
    hg = proj(OFF_HG, W_REC)
    g_ref[...] = (hg * _sigmoid(hg)).astype(BF16)

    qa_ref[...] = (seg_rms(proj(OFF_AQ, W_ATT)) * qn_ref[...]).astype(BF16)
    ka = seg_rms(proj(OFF_AK, W_ATT)) * kn_ref[...]
    store_head_rows(k32_ref, ka)
    k16_ref[...] = ka.astype(BF16)
    va = proj(OFF_AV, W_ATT)
    store_head_rows(v32_ref, va)
    if len(vt_ref.shape) == 2:
        vt_ref[...] = va.astype(BF16)
    else:
        kb = vt_ref.shape[2]
        for i in range(vt_ref.shape[0]):
            vt_ref[i] = va[i * kb:(i + 1) * kb, :].T.astype(BF16)
    ga_ref[...] = _sigmoid(proj(OFF_GA, D_MODEL)).astype(BF16)
    gb_ref[...] = _sigmoid(proj(OFF_GB, D_MODEL)).astype(BF16)


def _in_proj(x2d, n1, w_in16, lb, qn_row, kn_row, pseg, tm, kb):
    m = x2d.shape[0]
    row = lambda w: pl.BlockSpec((tm, w), lambda i: (i, 0))
    head_rows = pl.BlockSpec((tm * H_ATT, DV_ATT), lambda i: (i, 0))
    if kb is None:
        vt_spec, vt_shape = row(W_ATT), (m, W_ATT)
    else:
        vt_spec, vt_shape = pl.BlockSpec((tm // kb, W_ATT, kb), lambda i: (i, 0, 0)), (m // kb, W_ATT, kb)
    sds = jax.ShapeDtypeStruct
    return pl.pallas_call(
        _in_proj_kernel,
        grid=(m // tm,),
        in_specs=[row(D_MODEL), _const_spec((1, D_MODEL)), _const_spec((D_MODEL, IN_WIDTH)),
                  _const_spec((1, W_REC)), _const_spec((1, W_ATT)), _const_spec((1, W_ATT)),
                  _const_spec((W_ATT, W_ATT))],
        out_specs=[row(W_REC), row(W_REC), row(W_REC), row(W_REC), row(W_ATT), head_rows, row(W_ATT),
                   head_rows, vt_spec, row(D_MODEL), row(D_MODEL)],
        out_shape=[sds((m, W_REC), BF16), sds((m, W_REC), F32), sds((m, W_REC), BF16), sds((m, W_REC), BF16),
                   sds((m, W_ATT), BF16), sds((m * H_ATT, DV_ATT), F32), sds((m, W_ATT), BF16),
                   sds((m * H_ATT, DV_ATT), F32), sds(vt_shape, BF16),
                   sds((m, D_MODEL), BF16), sds((m, D_MODEL), BF16)],
        compiler_params=pltpu.CompilerParams(dimension_semantics=("parallel",),
                                             vmem_limit_bytes=V7X_VMEM_LIMIT_BYTES),
        name="in_proj",
    )(x2d, n1, w_in16, lb, qn_row, kn_row, pseg)


def _hgrn_kernel(q_ref, lf_ref, v_ref, g_ref, s0_ref, w_ref, y_ref, s_ref, st_scr, *, c, carry):
    r = q_ref.shape[1]
    nc = r // c
    t = pl.program_id(1)
    mm = BF16 if c >= 16 else F32
    shift = c.bit_length() - 1
    r_i = lax.broadcasted_iota(jnp.int32, (r, r), 0)
    c_i = lax.broadcasted_iota(jnp.int32, (r, r), 1)
    tril = ((r_i >> shift) == (c_i >> shift)) & (r_i >= c_i)
    tril16 = jnp.where(tril, 1.0, 0.0).astype(BF16)
    lane_chunk = lax.broadcasted_iota(jnp.int32, (DV_REC, r), 1) >> shift

    if carry:
        @pl.when(t == 0)
        def _():
            for h in range(H_REC):
                st_scr[h] = s0_ref[0, h].T

    for h in range(H_REC):
        cols = slice(h * DK_REC, (h + 1) * DK_REC)
        lf = lf_ref[0, :, cols]
        hi = lf.astype(BF16)
        lo = (lf - hi.astype(F32)).astype(BF16)
        b = jnp.dot(tril16, hi, preferred_element_type=F32) + jnp.dot(tril16, lo, preferred_element_type=F32)
        b3 = b.reshape(nc, c, DK_REC)
        bl = b3[:, c - 1:c, :]
        k = 1.0 - jnp.exp(lf)
        q = q_ref[0, :, cols].astype(F32)
        v16 = v_ref[0, :, cols]
        qe = q * jnp.exp(b)
        qe16 = qe.astype(BF16)
        ke16 = (k * jnp.exp(-b)).astype(BF16)
        kd16 = (k.reshape(nc, c, DK_REC) * jnp.exp(bl - b3)).reshape(r, DK_REC).astype(BF16)
        dec = jnp.exp(bl)
        a = lax.dot_general(qe16, ke16, _NT, preferred_element_type=F32)
        o = jnp.dot(jnp.where(tril, a, 0.0).astype(BF16), v16, preferred_element_type=F32)
        vt = v16.astype(F32).T
        lhs = jnp.concatenate([jnp.where(lane_chunk == n, vt, 0.0) for n in range(nc)], axis=0).astype(BF16)
        u = jnp.dot(lhs, kd16, preferred_element_type=F32)
        qe_mm = qe16 if mm == BF16 else qe
        parts = []
        if carry:
            st = st_scr[h]
            for n in range(nc):
                parts.append(lax.dot_general(qe_mm[n * c:(n + 1) * c], st.astype(mm), _NT,
                                             preferred_element_type=F32))
                st = st * dec[n] + u[n * DV_REC:(n + 1) * DV_REC]
            st_scr[h] = st
        else:
            for n in range(nc):
                st = s0_ref[n, h].T
                parts.append(lax.dot_general(qe_mm[n * c:(n + 1) * c], st.astype(mm), _NT,
                                             preferred_element_type=F32))
                s_ref[n, h] = (st * dec[n] + u[n * DV_REC:(n + 1) * DV_REC]).T
        o = o + jnp.concatenate(parts, axis=0)
        on = o * lax.rsqrt(jnp.mean(o * o, axis=-1, keepdims=True) + EPS) * w_ref[...]
        y_ref[0, :, cols] = (on * g_ref[0, :, cols].astype(F32)).astype(BF16)

    if carry:
        @pl.when(t == pl.num_programs(1) - 1)
        def _():
            for h in range(H_REC):
                s_ref[0, h] = st_scr[h].T


def _hgrn(q, lf, v, g, s0, w_row, rows, c, carry):
    gdim, t, _ = q.shape
    nseq = 1 if carry else rows // c
    blk = pl.BlockSpec((1, rows, W_REC), lambda gi, ti: (gi, ti, 0))
    if carry:
        sblk = pl.BlockSpec((1, H_REC, DK_REC, DV_REC), lambda gi, ti: (gi, 0, 0, 0))
    else:
        sblk = pl.BlockSpec((nseq, H_REC, DK_REC, DV_REC), lambda gi, ti: (ti, 0, 0, 0))
    return pl.pallas_call(
        functools.partial(_hgrn_kernel, c=c, carry=carry),
        grid=(gdim, t // rows),
        in_specs=[blk, blk, blk, blk, sblk, _const_spec((1, DV_REC))],
        out_specs=[blk, sblk],
        out_shape=[jax.ShapeDtypeStruct((gdim, t, W_REC), BF16), jax.ShapeDtypeStruct(s0.shape, F32)],
        scratch_shapes=[pltpu.VMEM((H_REC, DV_REC, DK_REC), F32)],
        compiler_params=pltpu.CompilerParams(dimension_semantics=("parallel", "arbitrary"),
                                             vmem_limit_bytes=V7X_VMEM_LIMIT_BYTES),
        name="hgrn",
    )(q, lf, v, g, s0, w_row)


def _rel_bucket(dist):
    dist = jnp.maximum(dist, 0)
    max_exact = NUM_BUCKETS // 2
    large = max_exact + (jnp.log(jnp.maximum(dist, 1).astype(F32) / max_exact)
                         / math.log(MAX_DISTANCE / max_exact) * (NUM_BUCKETS - max_exact)).astype(jnp.int32)
    large = jnp.minimum(large, NUM_BUCKETS - 1)
    return jnp.where(dist < max_exact, dist, large)


def _shifted_bias(rel_bias, dist):
    rb = rel_bias.astype(F32)
    rb = (rb - rb[NUM_BUCKETS - 1]) * LOG2E
    bucket = _rel_bucket(dist)[None]
    expand = (slice(None),) + (None,) * dist.ndim
    out = jnp.zeros((rb.shape[1],) + dist.shape, F32)
    for kb in range(NUM_BUCKETS - 1):
        out = jnp.where(bucket == kb, rb[kb][expand], out)
    return jnp.where((dist >= 0)[None], out, NEG)


def _split_halves(q):
    lane = lax.broadcasted_iota(jnp.int32, q.shape, 1)
    zero = jnp.zeros_like(q)
    return jnp.concatenate([jnp.where(lane < DK_ATT, q, zero), jnp.where(lane >= DK_ATT, q, zero)], axis=0)


def _attn_prompt_kernel(lam_ref, q_ref, k_ref, vt_ref, bd_ref, bs_ref, w_ref, o_ref, *, tq, lam_init):
    qi = pl.program_id(1)
    nt = 2 * tq // V7X_LANES
    for h in range(H_ATT):
        _attn_prompt_head(h, qi, nt, lam_ref, q_ref, k_ref, vt_ref, bd_ref, bs_ref, w_ref, o_ref, tq, lam_init)


def _attn_prompt_head(h, qi, nt, lam_ref, q_ref, k_ref, vt_ref, bd_ref, bs_ref, w_ref, o_ref, tq, lam_init):
    cols = slice(h * DV_ATT, (h + 1) * DV_ATT)
    qs = _split_halves(q_ref[0, :, cols])
    q_tiles = [qs[i * V7X_LANES:(i + 1) * V7X_LANES] for i in range(nt)]

    def scores(j):
        k_blk = k_ref[0, pl.ds(pl.multiple_of(j * tq, tq), tq), cols]
        return tuple(lax.dot_general(k_blk, q_tiles[i], _NT, preferred_element_type=F32) for i in range(nt))

    def update(j, s_tiles, bias_ref, extra, carry):
        vt_blk = vt_ref[j, cols, :]
        out = []
        for i in range(nt):
            m, l, acc = carry[i]
            s = s_tiles[i]
            if bias_ref is not None:
                c0 = (i * V7X_LANES) % tq
                s = s + bias_ref[h, :, c0:c0 + V7X_LANES]
                if extra is not None:
                    s = s + extra
            m_new = jnp.maximum(m, jnp.max(s, axis=0, keepdims=True))
            alpha = jnp.exp2(m - m_new)
            p = jnp.exp2(s - m_new)
            l = l * alpha + jnp.sum(p, axis=0, keepdims=True)
            acc = acc * alpha + jnp.dot(vt_blk, p.astype(BF16), preferred_element_type=F32)
            out.append((m_new, l, acc))
        return tuple(out)

    init = tuple((jnp.full((1, V7X_LANES), NEG, F32), jnp.zeros((1, V7X_LANES), F32),
                  jnp.zeros((DV_ATT, V7X_LANES), F32)) for _ in range(nt))
    sub_j = jnp.maximum(qi - 1, 0)
    n_far = jnp.maximum(qi - 1, 0)
    s_diag = scores(qi)
    s_sub = scores(sub_j)
    carry = update(qi, s_diag, bd_ref, None, init)
    s_far = scores(0)
    carry = update(sub_j, s_sub, bs_ref, jnp.where(qi == 0, NEG, 0.0), carry)

    def far_step(j, state):
        s_cur, cr = state
        s_next = scores(jnp.minimum(j + 1, qi))
        return s_next, update(j, s_cur, None, None, cr)

    _, carry = lax.fori_loop(0, n_far, far_step, (s_far, carry))
    o_t = jnp.concatenate([acc / l for _, l, acc in carry], axis=1)
    o_t = o_t[:, :tq] - lam_ref[0] * o_t[:, tq:]
    o_t = o_t * lax.rsqrt(jnp.mean(o_t * o_t, axis=0, keepdims=True) + EPS)
    o_ref[0, :, cols] = (o_t.T * w_ref[...] * (1.0 - lam_init)).astype(BF16)


def _attn_prompt(lam, qa, k16, vt, bias_diag, bias_sub, w_row, tq, lam_init):
    b, s, _ = qa.shape
    nb = s // tq
    assert tq >= MAX_DISTANCE and s % tq == 0
    return pl.pallas_call(
        functools.partial(_attn_prompt_kernel, tq=tq, lam_init=lam_init),
        grid=(b, nb),
        in_specs=[pl.BlockSpec(memory_space=pltpu.SMEM),
                  pl.BlockSpec((1, tq, W_ATT), lambda bi, qi: (bi, qi, 0)),
                  pl.BlockSpec((1, s, W_ATT), lambda bi, qi: (bi, 0, 0)),
                  pl.BlockSpec((nb, W_ATT, tq), lambda bi, qi: (bi, 0, 0)),
                  _const_spec((H_ATT, tq, tq)), _const_spec((H_ATT, tq, tq)), _const_spec((1, DV_ATT))],
        out_specs=pl.BlockSpec((1, tq, W_ATT), lambda bi, qi: (bi, qi, 0)),
        out_shape=jax.ShapeDtypeStruct((b, s, W_ATT), BF16),
        compiler_params=pltpu.CompilerParams(dimension_semantics=("parallel", "arbitrary")),
        name="attn_prompt",
    )(lam, qa, k16, vt, bias_diag, bias_sub, w_row)


def _attn_sample_kernel(pt_ref, lam_ref, wp_ref, kn_ref, vn_ref, bnew_ref, btail_ref, w_ref, *rest,
                        t_new, lam_init):
    npg = PAGES_PER_STEP
    k_refs, v_refs = rest[:npg], rest[npg:2 * npg]
    o_ref, m_scr, l_scr, acc_scr = rest[2 * npg:]
    j = pl.program_id(1)
    rh = 2 * t_new
    half = H_ATT * rh
    ncol = 2 * half
    wpair = wp_ref[0]

    def head_rows(ref, h):
        return ref[pl.ds(h, PAGE_SIZE, stride=H_ATT), :].astype(BF16)

    def rows_to_cols(row):
        return jnp.broadcast_to(row, (row.shape[1], row.shape[1])).T

    def weighted_values(p_t, get_v, parity):
        return [jnp.dot(p_t[parity * half + h * rh:parity * half + (h + 1) * rh], get_v(h),
                        preferred_element_type=F32) for h in range(H_ATT)]

    @pl.when(j == 0)
    def _():
        pad = PAGE_SIZE - t_new
        k_new = jnp.concatenate([kn_ref[0], jnp.zeros((pad, W_ATT), BF16)], axis=0)
        v_new = jnp.concatenate([vn_ref[0], jnp.zeros((pad, W_ATT), BF16)], axis=0)
        s = jnp.dot(k_new, wpair[:W_ATT, :], preferred_element_type=F32) + bnew_ref[...]
        m = jnp.max(s, axis=0, keepdims=True)
        even = lax.broadcasted_iota(jnp.int32, s.shape, 1) < half
        p = jnp.where(even, jnp.exp2(s - m), 0.0)
        m_scr[...] = jnp.where(even[:1], m, NEG)
        l_scr[...] = jnp.sum(p, axis=0, keepdims=True)
        p_t = p.T.astype(BF16)
        acc_scr[...] = jnp.concatenate(
            weighted_values(p_t, lambda h: v_new[:, h * DV_ATT:(h + 1) * DV_ATT], 0)
            + [jnp.zeros((half, DV_ATT), F32)], axis=0)

    tiles = []
    for pi in range(npg // 2):
        lhs = jnp.concatenate([head_rows(k_refs[2 * pi + par], h) for par in range(2) for h in range(H_ATT)],
                              axis=1)
        s = jnp.dot(lhs, wpair, preferred_element_type=F32)
        if pi == npg // 2 - 1:
            s = s + btail_ref[0]
        tiles.append(s)
    mx = tiles[0]
    for s in tiles[1:]:
        mx = jnp.maximum(mx, s)
    m_old = m_scr[...]
    m_new = jnp.maximum(m_old, jnp.max(mx, axis=0, keepdims=True))
    alpha = jnp.exp2(m_old - m_new)
    acc = acc_scr[...] * rows_to_cols(alpha)
    lsum = jnp.zeros_like(alpha)
    for pi in range(npg // 2):
        p = jnp.exp2(tiles[pi] - m_new)
        lsum = lsum + jnp.sum(p, axis=0, keepdims=True)
        p_t = p.T.astype(BF16)
        acc = acc + jnp.concatenate(
            [pv for par in range(2)
             for pv in weighted_values(p_t, functools.partial(head_rows, v_refs[2 * pi + par]), par)], axis=0)
    l = l_scr[...] * alpha + lsum
    m_scr[...] = m_new
    l_scr[...] = l
    acc_scr[...] = acc

    @pl.when(j == pl.num_programs(1) - 1)
    def _():
        m_c, l_c = rows_to_cols(m_new), rows_to_cols(l)
        m_f = jnp.maximum(m_c[:half], m_c[half:])
        w_e, w_o = jnp.exp2(m_c[:half] - m_f), jnp.exp2(m_c[half:] - m_f)
        o = (acc[:half] * w_e + acc[half:] * w_o) / (l_c[:half] * w_e + l_c[half:] * w_o)
        for h in range(H_ATT):
            oh = o[h * rh:h * rh + t_new] - lam_ref[0] * o[h * rh + t_new:(h + 1) * rh]
            oh = oh * lax.rsqrt(jnp.mean(oh * oh, axis=-1, keepdims=True) + EPS) * w_ref[...] * (1.0 - lam_init)
            o_ref[0, :, h * DV_ATT:(h + 1) * DV_ATT] = oh.astype(BF16)


def _attn_sample(page_table, lam, wpair, k_new, v_new, bias_new, bias_tail, w_row, cache_k, cache_v,
                 t_new, lam_init):
    nb, n_pages = page_table.shape
    npg = PAGES_PER_STEP
    assert n_pages % npg == 0 and npg % 2 == 0
    n_steps = n_pages // npg
    ncol = wpair.shape[2]
    prow = PAGE_SIZE * H_ATT

    def page_spec(i):
        return pl.BlockSpec((None, prow, DV_ATT), lambda b, j, pt: (pt[b, j * npg + i], 0, 0),
                            pipeline_mode=pl.Buffered(PAGE_BUFFERS))

    per_b = lambda n, w: pl.BlockSpec((1, n, w), lambda b, j, pt: (b, 0, 0))
    const = lambda shape: pl.BlockSpec(shape, lambda b, j, pt: (0,) * len(shape))
    grid_spec = pltpu.PrefetchScalarGridSpec(
        num_scalar_prefetch=1,
        grid=(nb, n_steps),
        in_specs=[pl.BlockSpec(memory_space=pltpu.SMEM), per_b(2 * W_ATT, ncol), per_b(t_new, W_ATT),
                  per_b(t_new, W_ATT), const((PAGE_SIZE, ncol)),
                  pl.BlockSpec((1, PAGE_SIZE, ncol), lambda b, j, pt: (jnp.where(j == n_steps - 1, 1, 0), 0, 0)),
                  const((1, DV_ATT))]
                 + [page_spec(i) for i in range(npg)] * 2,
        out_specs=pl.BlockSpec((1, t_new, W_ATT), lambda b, j, pt: (b, 0, 0)),
        scratch_shapes=[pltpu.VMEM((1, ncol), F32), pltpu.VMEM((1, ncol), F32), pltpu.VMEM((ncol, DV_ATT), F32)],
    )
    return pl.pallas_call(
        functools.partial(_attn_sample_kernel, t_new=t_new, lam_init=lam_init),
        grid_spec=grid_spec,
        out_shape=jax.ShapeDtypeStruct((nb, t_new, W_ATT), BF16),
        compiler_params=pltpu.CompilerParams(dimension_semantics=("parallel", "arbitrary"),
                                             vmem_limit_bytes=V7X_VMEM_LIMIT_BYTES),
        name="attn_sample",
    )(page_table, lam, wpair, k_new, v_new, bias_new, bias_tail, w_row,
      *([cache_k] * npg), *([cache_v] * npg))


def _out_mlp_kernel(x_ref, yr_ref, ya_ref, ga_ref, gb_ref, wa_ref, wb_ref, wo_ref, n2_ref, wu_ref, wd_ref,
                    y_ref, *, ff_chunk):
    a = jnp.dot(yr_ref[...], wa_ref[...], preferred_element_type=F32)
    b = jnp.dot(ya_ref[...], wb_ref[...], preferred_element_type=F32)
    mixed = ga_ref[...].astype(F32) * a + gb_ref[...].astype(F32) * b
    h = x_ref[...] + jnp.dot(mixed.astype(BF16), wo_ref[...], preferred_element_type=F32)
    ms = jnp.mean(h * h, axis=-1, keepdims=True)
    hn = (h * lax.rsqrt(ms + EPS) * n2_ref[...]).astype(BF16)
    y = h
    for c0 in range(0, D_FF, ff_chunk):
        u = jnp.maximum(jnp.dot(hn, wu_ref[:, c0:c0 + ff_chunk], preferred_element_type=F32), 0.0)
        y = y + jnp.dot((u * u).astype(BF16), wd_ref[c0:c0 + ff_chunk, :], preferred_element_type=F32)
    y_ref[...] = y


def _out_mlp(x2d, y_rec, y_att, ga, gb, wa16, wb16, wo16, n2, wu16, wd16, tm):
    m = x2d.shape[0]
    row = lambda w: pl.BlockSpec((tm, w), lambda i: (i, 0))
    return pl.pallas_call(
        functools.partial(_out_mlp_kernel, ff_chunk=D_MODEL),
        grid=(m // tm,),
        in_specs=[row(D_MODEL), row(W_REC), row(W_ATT), row(D_MODEL), row(D_MODEL),
                  _const_spec((W_REC, D_MODEL)), _const_spec((W_ATT, D_MODEL)), _const_spec((D_MODEL, D_MODEL)),
                  _const_spec((1, D_MODEL)), _const_spec((D_MODEL, D_FF)), _const_spec((D_FF, D_MODEL))],
        out_specs=row(D_MODEL),
        out_shape=jax.ShapeDtypeStruct((m, D_MODEL), F32),
        compiler_params=pltpu.CompilerParams(dimension_semantics=("parallel",),
                                             vmem_limit_bytes=V7X_VMEM_LIMIT_BYTES),
        name="out_mlp",
    )(x2d, y_rec, y_att, ga, gb, wa16, wb16, wo16, n2, wu16, wd16)


def kernel(x_prompt, x_sample, cache_k, cache_v, state_hgrn, page_table, norm1_w, w_in, lower_bounds, hgrn_norm_w, q_norm_w, k_norm_w, lam_q1, lam_k1, lam_q2, lam_k2, subln_w, rel_bias, w_branch_a, w_branch_b, w_out, norm2_w, w_up, w_down):
    depth = w_in.shape[0]
    assert depth == 1
    bp, sp, _ = x_prompt.shape
    bs, ts, _ = x_sample.shape
    n_pages = page_table.shape[1]
    past = n_pages * PAGE_SIZE
    assert PAGE_SIZE >= MAX_DISTANCE - 1 and ts <= HGRN_CHUNK
    l = 0
    lam_init = 0.8 - 0.6 * math.exp(-0.3 * l)
    scale = DK_ATT ** -0.5

    lb = jnp.cumsum(jax.nn.softmax(lower_bounds.astype(F32), axis=0), axis=0)[l][None, :]
    lam = (jnp.exp(jnp.dot(lam_q1[l].astype(F32), lam_k1[l].astype(F32)))
           - jnp.exp(jnp.dot(lam_q2[l].astype(F32), lam_k2[l].astype(F32))) + lam_init).reshape(1)
    n1 = norm1_w[l].astype(F32)[None, :]
    n2 = norm2_w[l].astype(F32)[None, :]
    qn_row = jnp.tile(q_norm_w[l].astype(F32).reshape(-1), H_ATT)[None, :] * (scale * LOG2E)
    kn_row = jnp.tile(k_norm_w[l].astype(F32).reshape(-1), H_ATT)[None, :]
    hw_row = hgrn_norm_w[l].astype(F32)[None, :]
    sw_row = subln_w[l].astype(F32)[None, :]
    seg = jnp.arange(W_ATT) // DK_ATT
    pseg = jnp.where(seg[:, None] == seg[None, :], 1.0 / DK_ATT, 0.0).astype(BF16)
    w_in16, wa16, wb16 = w_in[l].astype(BF16), w_branch_a[l].astype(BF16), w_branch_b[l].astype(BF16)
    wo16, wu16, wd16 = w_out[l].astype(BF16), w_up[l].astype(BF16), w_down[l].astype(BF16)

    def layer(x, attend, hgrn_fn, tm, kb):
        b, t, _ = x.shape
        x2d = x.reshape(b * t, D_MODEL)
        q, lf, vr, g, qa, k32, k16, v32, vt, ga, gb = _in_proj(x2d, n1, w_in16, lb, qn_row, kn_row, pseg, tm, kb)
        y_rec, s_new = hgrn_fn(q, lf, vr, g)
        y_att = attend(qa, k16, vt)
        y = _out_mlp(x2d, y_rec.reshape(b * t, W_REC), y_att.reshape(b * t, W_ATT), ga, gb,
                     wa16, wb16, wo16, n2, wu16, wd16, tm)
        return (y.reshape(b, t, D_MODEL), k32.reshape(1, b, t, H_ATT, DV_ATT),
                v32.reshape(1, b, t, H_ATT, DV_ATT), s_new[None])

    tq = ATT_BLOCK
    ri = jnp.arange(tq)
    bias_diag = _shifted_bias(rel_bias, ri[None, :] - ri[:, None])
    bias_sub = _shifted_bias(rel_bias, ri[None, :] + tq - ri[:, None])
    r3p = lambda a: a.reshape(bp, sp, a.shape[-1])
    attend_p = lambda qa, k16, vt: _attn_prompt(lam, r3p(qa), r3p(k16), vt, bias_diag, bias_sub, sw_row, tq, lam_init)
    s0p = jnp.zeros((bp, H_REC, DK_REC, DV_REC), F32)
    hgrn_p = lambda q, lf, vr, g: _hgrn(r3p(q), r3p(lf), r3p(vr), r3p(g), s0p, hw_row, 256, HGRN_CHUNK, True)
    y_p, k_p, v_p, s_p = layer(x_prompt, attend_p, hgrn_p, 512, tq)

    half = H_ATT * 2 * ts
    col_h = jnp.arange(half) // (2 * ts)
    col_t = jnp.arange(half) % ts
    assert 2 * half == V7X_LANES and n_pages % 2 == 0

    def own_head(bias):
        out = jnp.zeros(bias.shape[1:], F32)
        for h in range(H_ATT):
            out = jnp.where((col_h == h)[None, :], bias[h], out)
        return out

    keys = jnp.arange(PAGE_SIZE)
    tail_dist = (past + col_t)[None, :] - ((n_pages - 1) * PAGE_SIZE + keys)[:, None]
    zeros_half = jnp.zeros((PAGE_SIZE, half), F32)
    bias_tail = jnp.stack([jnp.zeros((PAGE_SIZE, 2 * half), F32),
                           jnp.concatenate([zeros_half, own_head(_shifted_bias(rel_bias, tail_dist))], axis=1)])
    new_dist = jnp.where((keys < ts)[:, None], col_t[None, :] - keys[:, None], -1)
    bias_new = jnp.concatenate([own_head(_shifted_bias(rel_bias, new_dist)), zeros_half], axis=1)
    ck = cache_k[l].reshape(cache_k.shape[1], PAGE_SIZE * H_ATT, DV_ATT)
    cv = cache_v[l].reshape(cache_v.shape[1], PAGE_SIZE * H_ATT, DV_ATT)

    def attend_s(qa, k16, vt):
        q4 = qa.reshape(bs, ts, H_ATT, DV_ATT).transpose(0, 2, 3, 1)
        d = jnp.arange(DV_ATT)[:, None]
        qcols = jnp.concatenate([jnp.where(d < DK_ATT, q4, 0), jnp.where(d >= DK_ATT, q4, 0)], axis=3)
        eye = jnp.eye(H_ATT, dtype=qcols.dtype)
        wstack = (qcols[:, :, :, None, :] * eye[None, :, None, :, None]).reshape(bs, W_ATT, half)
        zero = jnp.zeros_like(wstack)
        wpair = jnp.concatenate([jnp.concatenate([wstack, zero], axis=2),
                                 jnp.concatenate([zero, wstack], axis=2)], axis=1).astype(BF16)
        return _attn_sample(page_table, lam, wpair, k16.reshape(bs, ts, W_ATT), vt.reshape(bs, ts, W_ATT),
                            bias_new, bias_tail, sw_row, ck, cv, ts, lam_init)

    r3s = lambda a: a.reshape(1, bs * ts, a.shape[-1])
    hgrn_rows = 16 * ts
    hgrn_s = lambda q, lf, vr, g: _hgrn(r3s(q), r3s(lf), r3s(vr), r3s(g), state_hgrn[l].astype(F32), hw_row,
                                        hgrn_rows, ts, False)
    y_s, k_s, v_s, s_s = layer(x_sample, attend_s, hgrn_s, 512, None)
    return (y_p, y_s, k_p, v_p, s_p, k_s, v_s, s_s)
```

```python
import functools
import math

import jax
import jax.numpy as jnp
from jax import lax
from jax.experimental import pallas as pl
from jax.experimental.pallas import tpu as pltpu

F32 = jnp.float32
BF16 = jnp.bfloat16

D_MODEL = 1024
H_REC, DK_REC, DV_REC = 4, 128, 128
W_REC = H_REC * DK_REC
H_ATT, DK_ATT = 4, 64
DV_ATT = 2 * DK_ATT
W_ATT = H_ATT * DV_ATT
D_FF = 4 * D_MODEL
NUM_BUCKETS = 32
MAX_DISTANCE = 128
PAGE_SIZE = 128
HGRN_CHUNK = 32
EPS = 1e-6
NEG = -1e30
LOG2E = math.log2(math.e)

OFF_HQ, OFF_HF, OFF_HI, OFF_HG = 0, W_REC, 2 * W_REC, 3 * W_REC
OFF_AQ = 4 * W_REC
OFF_AK = OFF_AQ + W_ATT
OFF_AV = OFF_AK + W_ATT
OFF_GA = OFF_AV + W_ATT
OFF_GB = OFF_GA + D_MODEL
IN_WIDTH = OFF_GB + D_MODEL

V7X_VMEM_LIMIT_BYTES = 56 * 1024 * 1024
V7X_LANES = 128
PAGES_PER_STEP = 16
PAGE_BUFFERS = 2
ATT_BLOCK = 256
ATT_TILE = 128
ATT_HEAD_GROUP = 4

_NT = (((1,), (1,)), ((), ()))


def _const_spec(shape):
    n = len(shape)
    return pl.BlockSpec(shape, lambda *_: (0,) * n, pipeline_mode=pl.Buffered(1))


def _sigmoid(x):
    return 1.0 / (1.0 + jnp.exp(-x))


def _in_proj_kernel(x_ref, n1_ref, w_ref, lb_ref, qn_ref, kn_ref, pseg_ref,
                    q_ref, lf_ref, vr_ref, g_ref, qa_ref, k32_ref, k16_ref, v32_ref, vt_ref,
                    ga_ref, gb_ref):
    tm = x_ref.shape[0]
    x = x_ref[...]
    ms = jnp.mean(x * x, axis=-1, keepdims=True)
    xb = (x * lax.rsqrt(ms + EPS) * n1_ref[...]).astype(BF16)

    def proj(off, size):
        return jnp.dot(xb, w_ref[:, off:off + size], preferred_element_type=F32)

    def seg_rms(z):
        m = jnp.dot((z * z).astype(BF16), pseg_ref[...], preferred_element_type=F32)
        return z * lax.rsqrt(m + EPS)

    def store_head_rows(ref, z):
        for h in range(H_ATT):
            ref[pl.ds(h, tm, stride=H_ATT), :] = z[:, h * DV_ATT:(h + 1) * DV_ATT]

    hq = proj(OFF_HQ, W_REC)
    q_ref[...] = (hq * _sigmoid(hq)).astype(BF16)
    hf = proj(OFF_HF, W_REC)
    lb = lb_ref[...]
    lf_ref[...] = jnp.log(lb + (1.0 - lb) * _sigmoid(hf))
    vr_ref[...] = proj(OFF_HI, W_REC).astype(BF16)
    hg = proj(OFF_HG, W_REC)
    g_ref[...] = (hg * _sigmoid(hg)).astype(BF16)

    qa_ref[...] = (seg_rms(proj(OFF_AQ, W_ATT)) * qn_ref[...]).astype(BF16)
    ka = seg_rms(proj(OFF_AK, W_ATT)) * kn_ref[...]
    store_head_rows(k32_ref, ka)
    k16_ref[...] = ka.astype(BF16)
    va = proj(OFF_AV, W_ATT)
    store_head_rows(v32_ref, va)
    if len(vt_ref.shape) == 2:
        vt_ref[...] = va.astype(BF16)
    else:
        kb = vt_ref.shape[2]
        for i in range(vt_ref.shape[0]):
            vt_ref[i] = va[i * kb:(i + 1) * kb, :].T.astype(BF16)
    ga_ref[...] = _sigmoid(proj(OFF_GA, D_MODEL)).astype(BF16)
    gb_ref[...] = _sigmoid(proj(OFF_GB, D_MODEL)).astype(BF16)


def _in_proj(x2d, n1, w_in16, lb, qn_row, kn_row, pseg, tm, kb):
    m = x2d.shape[0]
    row = lambda w: pl.BlockSpec((tm, w), lambda i: (i, 0))
    head_rows = pl.BlockSpec((tm * H_ATT, DV_ATT), lambda i: (i, 0))
    if kb is None:
        vt_spec, vt_shape = row(W_ATT), (m, W_ATT)
    else:
        vt_spec, vt_shape = pl.BlockSpec((tm // kb, W_ATT, kb), lambda i: (i, 0, 0)), (m // kb, W_ATT, kb)
    sds = jax.ShapeDtypeStruct
    return pl.pallas_call(
        _in_proj_kernel,
        grid=(m // tm,),
        in_specs=[row(D_MODEL), _const_spec((1, D_MODEL)), _const_spec((D_MODEL, IN_WIDTH)),
                  _const_spec((1, W_REC)), _const_spec((1, W_ATT)), _const_spec((1, W_ATT)),
                  _const_spec((W_ATT, W_ATT))],
        out_specs=[row(W_REC), row(W_REC), row(W_REC), row(W_REC), row(W_ATT), head_rows, row(W_ATT),
                   head_rows, vt_spec, row(D_MODEL), row(D_MODEL)],
        out_shape=[sds((m, W_REC), BF16), sds((m, W_REC), F32), sds((m, W_REC), BF16), sds((m, W_REC), BF16),
                   sds((m, W_ATT), BF16), sds((m * H_ATT, DV_ATT), F32), sds((m, W_ATT), BF16),
                   sds((m * H_ATT, DV_ATT), F32), sds(vt_shape, BF16),
                   sds((m, D_MODEL), BF16), sds((m, D_MODEL), BF16)],
        compiler_params=pltpu.CompilerParams(dimension_semantics=("parallel",),
                                             vmem_limit_bytes=V7X_VMEM_LIMIT_BYTES),
        name="in_proj",
    )(x2d, n1, w_in16, lb, qn_row, kn_row, pseg)


def _hgrn_kernel(q_ref, lf_ref, v_ref, g_ref, s0_ref, w_ref, y_ref, s_ref, st_scr, *, c, carry):
    r = q_ref.shape[1]
    nc = r // c
    t = pl.program_id(1)
    mm = BF16 if c >= 16 else F32
    shift = c.bit_length() - 1
    r_i = lax.broadcasted_iota(jnp.int32, (r, r), 0)
    c_i = lax.broadcasted_iota(jnp.int32, (r, r), 1)
    tril = ((r_i >> shift) == (c_i >> shift)) & (r_i >= c_i)
    tril16 = jnp.where(tril, 1.0, 0.0).astype(BF16)
    lane_chunk = lax.broadcasted_iota(jnp.int32, (DV_REC, r), 1) >> shift

    if carry:
        @pl.when(t == 0)
        def _():
            for h in range(H_REC):
                st_scr[h] = s0_ref[0, h].T

    for h in range(H_REC):
        cols = slice(h * DK_REC, (h + 1) * DK_REC)
        lf = lf_ref[0, :, cols]
        hi = lf.astype(BF16)
        lo = (lf - hi.astype(F32)).astype(BF16)
        b = jnp.dot(tril16, hi, preferred_element_type=F32) + jnp.dot(tril16, lo, preferred_element_type=F32)
        b3 = b.reshape(nc, c, DK_REC)
        bl = b3[:, c - 1:c, :]
        k = 1.0 - jnp.exp(lf)
        q = q_ref[0, :, cols].astype(F32)
        v16 = v_ref[0, :, cols]
        qe = q * jnp.exp(b)
        qe16 = qe.astype(BF16)
        ke16 = (k * jnp.exp(-b)).astype(BF16)
        kd16 = (k.reshape(nc, c, DK_REC) * jnp.exp(bl - b3)).reshape(r, DK_REC).astype(BF16)
        dec = jnp.exp(bl)
        a = lax.dot_general(qe16, ke16, _NT, preferred_element_type=F32)
        o = jnp.dot(jnp.where(tril, a, 0.0).astype(BF16), v16, preferred_element_type=F32)
        vt = v16.astype(F32).T
        lhs = jnp.concatenate([jnp.where(lane_chunk == n, vt, 0.0) for n in range(nc)], axis=0).astype(BF16)
        u = jnp.dot(lhs, kd16, preferred_element_type=F32)
        qe_mm = qe16 if mm == BF16 else qe
        parts = []
        if carry:
            st = st_scr[h]
            for n in range(nc):
                parts.append(lax.dot_general(qe_mm[n * c:(n + 1) * c], st.astype(mm), _NT,
                                             preferred_element_type=F32))
                st = st * dec[n] + u[n * DV_REC:(n + 1) * DV_REC]
            st_scr[h] = st
        else:
            for n in range(nc):
                st = s0_ref[n, h].T
                parts.append(lax.dot_general(qe_mm[n * c:(n + 1) * c], st.astype(mm), _NT,
                                             preferred_element_type=F32))
                s_ref[n, h] = (st * dec[n] + u[n * DV_REC:(n + 1) * DV_REC]).T
        o = o + jnp.concatenate(parts, axis=0)
        on = o * lax.rsqrt(jnp.mean(o * o, axis=-1, keepdims=True) + EPS) * w_ref[...]
        y_ref[0, :, cols] = (on * g_ref[0, :, cols].astype(F32)).astype(BF16)

    if carry:
        @pl.when(t == pl.num_programs(1) - 1)
        def _():
            for h in range(H_REC):
                s_ref[0, h] = st_scr[h].T


def _hgrn(q, lf, v, g, s0, w_row, rows, c, carry):
    gdim, t, _ = q.shape
    nseq = 1 if carry else rows // c
    blk = pl.BlockSpec((1, rows, W_REC), lambda gi, ti: (gi, ti, 0))
    if carry:
        sblk = pl.BlockSpec((1, H_REC, DK_REC, DV_REC), lambda gi, ti: (gi, 0, 0, 0))
    else:
        sblk = pl.BlockSpec((nseq, H_REC, DK_REC, DV_REC), lambda gi, ti: (ti, 0, 0, 0))
    return pl.pallas_call(
        functools.partial(_hgrn_kernel, c=c, carry=carry),
        grid=(gdim, t // rows),
        in_specs=[blk, blk, blk, blk, sblk, _const_spec((1, DV_REC))],
        out_specs=[blk, sblk],
        out_shape=[jax.ShapeDtypeStruct((gdim, t, W_REC), BF16), jax.ShapeDtypeStruct(s0.shape, F32)],
        scratch_shapes=[pltpu.VMEM((H_REC, DV_REC, DK_REC), F32)],
        compiler_params=pltpu.CompilerParams(dimension_semantics=("parallel", "arbitrary"),
                                             vmem_limit_bytes=V7X_VMEM_LIMIT_BYTES),
        name="hgrn",
    )(q, lf, v, g, s0, w_row)


def _rel_bucket(dist):
    dist = jnp.maximum(dist, 0)
    max_exact = NUM_BUCKETS // 2
    large = max_exact + (jnp.log(jnp.maximum(dist, 1).astype(F32) / max_exact)
                         / math.log(MAX_DISTANCE / max_exact) * (NUM_BUCKETS - max_exact)).astype(jnp.int32)
    large = jnp.minimum(large, NUM_BUCKETS - 1)
    return jnp.where(dist < max_exact, dist, large)


def _shifted_bias(rel_bias, dist):
    rb = rel_bias.astype(F32)
    rb = (rb - rb[NUM_BUCKETS - 1]) * LOG2E
    bucket = _rel_bucket(dist)[None]
    expand = (slice(None),) + (None,) * dist.ndim
    out = jnp.zeros((rb.shape[1],) + dist.shape, F32)
    for kb in range(NUM_BUCKETS - 1):
        out = jnp.where(bucket == kb, rb[kb][expand], out)
    return jnp.where((dist >= 0)[None], out, NEG)


def _split_halves(q):
    lane = lax.broadcasted_iota(jnp.int32, q.shape, 1)
    zero = jnp.zeros_like(q)
    return jnp.concatenate([jnp.where(lane < DK_ATT, q, zero), jnp.where(lane >= DK_ATT, q, zero)], axis=0)


def _attn_prompt_kernel(lam_ref, q_ref, k_ref, vt_ref, bd_ref, bs_ref, w_ref, o_ref, *, tq, lam_init):
    qi = pl.program_id(1)
    for h0 in range(0, H_ATT, ATT_HEAD_GROUP):
        _attn_prompt_heads(range(h0, h0 + ATT_HEAD_GROUP), qi, lam_ref, q_ref, k_ref, vt_ref, bd_ref, bs_ref,
                           w_ref, o_ref, tq, lam_init)


def _attn_prompt_heads(heads, qi, lam_ref, q_ref, k_ref, vt_ref, bd_ref, bs_ref, w_ref, o_ref, tq, lam_init):
    nt = 2 * tq // ATT_TILE
    cols = {h: slice(h * DV_ATT, (h + 1) * DV_ATT) for h in heads}
    q_tiles = {}
    for h in heads:
        qs = _split_halves(q_ref[0, :, cols[h]])
        q_tiles[h] = [qs[i * ATT_TILE:(i + 1) * ATT_TILE] for i in range(nt)]

    def scores(j):
        out = []
        for h in heads:
            k_blk = k_ref[0, pl.ds(pl.multiple_of(j * tq, tq), tq), cols[h]]
            out.append(tuple(lax.dot_general(k_blk, q_tiles[h][i], _NT, preferred_element_type=F32)
                             for i in range(nt)))
        return tuple(out)

    def update(j, s_all, bias_ref, extra, carry):
        out = []
        for g, h in enumerate(heads):
            vt_blk = vt_ref[j, cols[h], :]
            tiles = []
            for i in range(nt):
                m, l, acc = carry[g][i]
                s = s_all[g][i]
                if bias_ref is not None:
                    c0 = (i * ATT_TILE) % tq
                    s = s + bias_ref[h, :, c0:c0 + ATT_TILE]
                    if extra is not None:
                        s = s + extra
                m_new = jnp.maximum(m, jnp.max(s, axis=0, keepdims=True))
                alpha = jnp.exp2(m - m_new)
                p = jnp.exp2(s - m_new)
                l = l * alpha + jnp.sum(p, axis=0, keepdims=True)
                acc = acc * alpha + jnp.dot(vt_blk, p.astype(BF16), preferred_element_type=F32)
                tiles.append((m_new, l, acc))
            out.append(tuple(tiles))
        return tuple(out)

    init = tuple(tuple((jnp.full((1, ATT_TILE), NEG, F32), jnp.zeros((1, ATT_TILE), F32),
                        jnp.zeros((DV_ATT, ATT_TILE), F32)) for _ in range(nt)) for _ in heads)
    sub_j = jnp.maximum(qi - 1, 0)
    n_far = jnp.maximum(qi - 1, 0)
    s_diag = scores(qi)
    s_sub = scores(sub_j)
    carry = update(qi, s_diag, bd_ref, None, init)
    carry = update(sub_j, s_sub, bs_ref, jnp.where(qi == 0, NEG, 0.0), carry)

    def far_step(j, cr):
        return update(j, scores(j), None, None, cr)

    carry = lax.fori_loop(0, n_far, far_step, carry)
    for g, h in enumerate(heads):
        o_t = jnp.concatenate([acc / l for _, l, acc in carry[g]], axis=1)
        o_t = o_t[:, :tq] - lam_ref[0] * o_t[:, tq:]
        o_t = o_t * lax.rsqrt(jnp.mean(o_t * o_t, axis=0, keepdims=True) + EPS)
        o_ref[0, :, cols[h]] = (o_t.T * w_ref[...] * (1.0 - lam_init)).astype(BF16)


def _attn_prompt(lam, qa, k16, vt, bias_diag, bias_sub, w_row, tq, lam_init):
    b, s, _ = qa.shape
    nb = s // tq
    assert tq >= MAX_DISTANCE and s % tq == 0
    return pl.pallas_call(
        functools.partial(_attn_prompt_kernel, tq=tq, lam_init=lam_init),
        grid=(b, nb),
        in_specs=[pl.BlockSpec(memory_space=pltpu.SMEM),
                  pl.BlockSpec((1, tq, W_ATT), lambda bi, qi: (bi, qi, 0)),
                  pl.BlockSpec((1, s, W_ATT), lambda bi, qi: (bi, 0, 0)),
                  pl.BlockSpec((nb, W_ATT, tq), lambda bi, qi: (bi, 0, 0)),
                  _const_spec((H_ATT, tq, tq)), _const_spec((H_ATT, tq, tq)), _const_spec((1, DV_ATT))],
        out_specs=pl.BlockSpec((1, tq, W_ATT), lambda bi, qi: (bi, qi, 0)),
        out_shape=jax.ShapeDtypeStruct((b, s, W_ATT), BF16),
        compiler_params=pltpu.CompilerParams(dimension_semantics=("parallel", "arbitrary")),
        name="attn_prompt",
    )(lam, qa, k16, vt, bias_diag, bias_sub, w_row)


def _attn_sample_kernel(pt_ref, lam_ref, wp_ref, kn_ref, vn_ref, bnew_ref, btail_ref, w_ref, *rest,
                        t_new, lam_init):
    npg = PAGES_PER_STEP
    k_refs, v_refs = rest[:npg], rest[npg:2 * npg]
    o_ref, m_scr, l_scr, acc_scr = rest[2 * npg:]
    j = pl.program_id(1)
    rh = 2 * t_new
    half = H_ATT * rh
    ncol = 2 * half
    wpair = wp_ref[0]

    def head_rows(ref, h):
        return ref[pl.ds(h, PAGE_SIZE, stride=H_ATT), :].astype(BF16)

    def rows_to_cols(row):
        return jnp.broadcast_to(row, (row.shape[1], row.shape[1])).T

    def weighted_values(p_t, get_v, parity):
        return [jnp.dot(p_t[parity * half + h * rh:parity * half + (h + 1) * rh], get_v(h),
                        preferred_element_type=F32) for h in range(H_ATT)]

    @pl.when(j == 0)
    def _():
        pad = PAGE_SIZE - t_new
        k_new = jnp.concatenate([kn_ref[0], jnp.zeros((pad, W_ATT), BF16)], axis=0)
        v_new = jnp.concatenate([vn_ref[0], jnp.zeros((pad, W_ATT), BF16)], axis=0)
        s = jnp.dot(k_new, wpair[:W_ATT, :], preferred_element_type=F32) + bnew_ref[...]
        m = jnp.max(s, axis=0, keepdims=True)
        even = lax.broadcasted_iota(jnp.int32, s.shape, 1) < half
        p = jnp.where(even, jnp.exp2(s - m), 0.0)
        m_scr[...] = jnp.where(even[:1], m, NEG)
        l_scr[...] = jnp.sum(p, axis=0, keepdims=True)
        p_t = p.T.astype(BF16)
        acc_scr[...] = jnp.concatenate(
            weighted_values(p_t, lambda h: v_new[:, h * DV_ATT:(h + 1) * DV_ATT], 0)
            + [jnp.zeros((half, DV_ATT), F32)], axis=0)

    tiles = []
    for pi in range(npg // 2):
        lhs = jnp.concatenate([head_rows(k_refs[2 * pi + par], h) for par in range(2) for h in range(H_ATT)],
                              axis=1)
        s = jnp.dot(lhs, wpair, preferred_element_type=F32)
        if pi == npg // 2 - 1:
            s = s + btail_ref[0]
        tiles.append(s)
    mx = tiles[0]
    for s in tiles[1:]:
        mx = jnp.maximum(mx, s)
    m_old = m_scr[...]
    m_new = jnp.maximum(m_old, jnp.max(mx, axis=0, keepdims=True))
    alpha = jnp.exp2(m_old - m_new)
    acc = acc_scr[...] * rows_to_cols(alpha)
    lsum = jnp.zeros_like(alpha)
    for pi in range(npg // 2):
        p = jnp.exp2(tiles[pi] - m_new)
        lsum = lsum + jnp.sum(p, axis=0, keepdims=True)
        p_t = p.T.astype(BF16)
        acc = acc + jnp.concatenate(
            [pv for par in range(2)
             for pv in weighted_values(p_t, functools.partial(head_rows, v_refs[2 * pi + par]), par)], axis=0)
    l = l_scr[...] * alpha + lsum
    m_scr[...] = m_new
    l_scr[...] = l
    acc_scr[...] = acc

    @pl.when(j == pl.num_programs(1) - 1)
    def _():
        m_c, l_c = rows_to_cols(m_new), rows_to_cols(l)
        m_f = jnp.maximum(m_c[:half], m_c[half:])
        w_e, w_o = jnp.exp2(m_c[:half] - m_f), jnp.exp2(m_c[half:] - m_f)
        o = (acc[:half] * w_e + acc[half:] * w_o) / (l_c[:half] * w_e + l_c[half:] * w_o)
        for h in range(H_ATT):
            oh = o[h * rh:h * rh + t_new] - lam_ref[0] * o[h * rh + t_new:(h + 1) * rh]
            oh = oh * lax.rsqrt(jnp.mean(oh * oh, axis=-1, keepdims=True) + EPS) * w_ref[...] * (1.0 - lam_init)
            o_ref[0, :, h * DV_ATT:(h + 1) * DV_ATT] = oh.astype(BF16)


def _attn_sample(page_table, lam, wpair, k_new, v_new, bias_new, bias_tail, w_row, cache_k, cache_v,
                 t_new, lam_init):
    nb, n_pages = page_table.shape
    npg = PAGES_PER_STEP
    assert n_pages % npg == 0 and npg % 2 == 0
    n_steps = n_pages // npg
    ncol = wpair.shape[2]
    prow = PAGE_SIZE * H_ATT

    def page_spec(i):
        return pl.BlockSpec((None, prow, DV_ATT), lambda b, j, pt: (pt[b, j * npg + i], 0, 0),
                            pipeline_mode=pl.Buffered(PAGE_BUFFERS))

    per_b = lambda n, w: pl.BlockSpec((1, n, w), lambda b, j, pt: (b, 0, 0))
    const = lambda shape: pl.BlockSpec(shape, lambda b, j, pt: (0,) * len(shape))
    grid_spec = pltpu.PrefetchScalarGridSpec(
        num_scalar_prefetch=1,
        grid=(nb, n_steps),
        in_specs=[pl.BlockSpec(memory_space=pltpu.SMEM), per_b(2 * W_ATT, ncol), per_b(t_new, W_ATT),
                  per_b(t_new, W_ATT), const((PAGE_SIZE, ncol)),
                  pl.BlockSpec((1, PAGE_SIZE, ncol), lambda b, j, pt: (jnp.where(j == n_steps - 1, 1, 0), 0, 0)),
                  const((1, DV_ATT))]
                 + [page_spec(i) for i in range(npg)] * 2,
        out_specs=pl.BlockSpec((1, t_new, W_ATT), lambda b, j, pt: (b, 0, 0)),
        scratch_shapes=[pltpu.VMEM((1, ncol), F32), pltpu.VMEM((1, ncol), F32), pltpu.VMEM((ncol, DV_ATT), F32)],
    )
    return pl.pallas_call(
        functools.partial(_attn_sample_kernel, t_new=t_new, lam_init=lam_init),
        grid_spec=grid_spec,
        out_shape=jax.ShapeDtypeStruct((nb, t_new, W_ATT), BF16),
        compiler_params=pltpu.CompilerParams(dimension_semantics=("parallel", "arbitrary"),
                                             vmem_limit_bytes=V7X_VMEM_LIMIT_BYTES),
        name="attn_sample",
    )(page_table, lam, wpair, k_new, v_new, bias_new, bias_tail, w_row,
      *([cache_k] * npg), *([cache_v] * npg))


def _out_mlp_kernel(x_ref, yr_ref, ya_ref, ga_ref, gb_ref, wa_ref, wb_ref, wo_ref, n2_ref, wu_ref, wd_ref,
                    y_ref, *, ff_chunk):
    a = jnp.dot(yr_ref[...], wa_ref[...], preferred_element_type=F32)
    b = jnp.dot(ya_ref[...], wb_ref[...], preferred_element_type=F32)
    mixed = ga_ref[...].astype(F32) * a + gb_ref[...].astype(F32) * b
    h = x_ref[...] + jnp.dot(mixed.astype(BF16), wo_ref[...], preferred_element_type=F32)
    ms = jnp.mean(h * h, axis=-1, keepdims=True)
    hn = (h * lax.rsqrt(ms + EPS) * n2_ref[...]).astype(BF16)
    y = h
    for c0 in range(0, D_FF, ff_chunk):
        u = jnp.maximum(jnp.dot(hn, wu_ref[:, c0:c0 + ff_chunk], preferred_element_type=F32), 0.0)
        y = y + jnp.dot((u * u).astype(BF16), wd_ref[c0:c0 + ff_chunk, :], preferred_element_type=F32)
    y_ref[...] = y


def _out_mlp(x2d, y_rec, y_att, ga, gb, wa16, wb16, wo16, n2, wu16, wd16, tm):
    m = x2d.shape[0]
    row = lambda w: pl.BlockSpec((tm, w), lambda i: (i, 0))
    return pl.pallas_call(
        functools.partial(_out_mlp_kernel, ff_chunk=D_MODEL),
        grid=(m // tm,),
        in_specs=[row(D_MODEL), row(W_REC), row(W_ATT), row(D_MODEL), row(D_MODEL),
                  _const_spec((W_REC, D_MODEL)), _const_spec((W_ATT, D_MODEL)), _const_spec((D_MODEL, D_MODEL)),
                  _const_spec((1, D_MODEL)), _const_spec((D_MODEL, D_FF)), _const_spec((D_FF, D_MODEL))],
        out_specs=row(D_MODEL),
        out_shape=jax.ShapeDtypeStruct((m, D_MODEL), F32),
        compiler_params=pltpu.CompilerParams(dimension_semantics=("parallel",),
                                             vmem_limit_bytes=V7X_VMEM_LIMIT_BYTES),
        name="out_mlp",
    )(x2d, y_rec, y_att, ga, gb, wa16, wb16, wo16, n2, wu16, wd16)


def kernel(x_prompt, x_sample, cache_k, cache_v, state_hgrn, page_table, norm1_w, w_in, lower_bounds, hgrn_norm_w, q_norm_w, k_norm_w, lam_q1, lam_k1, lam_q2, lam_k2, subln_w, rel_bias, w_branch_a, w_branch_b, w_out, norm2_w, w_up, w_down):
    depth = w_in.shape[0]
    assert depth == 1
    bp, sp, _ = x_prompt.shape
    bs, ts, _ = x_sample.shape
    n_pages = page_table.shape[1]
    past = n_pages * PAGE_SIZE
    assert PAGE_SIZE >= MAX_DISTANCE - 1 and ts <= HGRN_CHUNK
    l = 0
    lam_init = 0.8 - 0.6 * math.exp(-0.3 * l)
    scale = DK_ATT ** -0.5

    lb = jnp.cumsum(jax.nn.softmax(lower_bounds.astype(F32), axis=0), axis=0)[l][None, :]
    lam = (jnp.exp(jnp.dot(lam_q1[l].astype(F32), lam_k1[l].astype(F32)))
           - jnp.exp(jnp.dot(lam_q2[l].astype(F32), lam_k2[l].astype(F32))) + lam_init).reshape(1)
    n1 = norm1_w[l].astype(F32)[None, :]
    n2 = norm2_w[l].astype(F32)[None, :]
    qn_row = jnp.tile(q_norm_w[l].astype(F32).reshape(-1), H_ATT)[None, :] * (scale * LOG2E)
    kn_row = jnp.tile(k_norm_w[l].astype(F32).reshape(-1), H_ATT)[None, :]
    hw_row = hgrn_norm_w[l].astype(F32)[None, :]
    sw_row = subln_w[l].astype(F32)[None, :]
    seg = jnp.arange(W_ATT) // DK_ATT
    pseg = jnp.where(seg[:, None] == seg[None, :], 1.0 / DK_ATT, 0.0).astype(BF16)
    w_in16, wa16, wb16 = w_in[l].astype(BF16), w_branch_a[l].astype(BF16), w_branch_b[l].astype(BF16)
    wo16, wu16, wd16 = w_out[l].astype(BF16), w_up[l].astype(BF16), w_down[l].astype(BF16)

    def layer(x, attend, hgrn_fn, tm, kb):
        b, t, _ = x.shape
        x2d = x.reshape(b * t, D_MODEL)
        q, lf, vr, g, qa, k32, k16, v32, vt, ga, gb = _in_proj(x2d, n1, w_in16, lb, qn_row, kn_row, pseg, tm, kb)
        y_rec, s_new = hgrn_fn(q, lf, vr, g)
        y_att = attend(qa, k16, vt)
        y = _out_mlp(x2d, y_rec.reshape(b * t, W_REC), y_att.reshape(b * t, W_ATT), ga, gb,
                     wa16, wb16, wo16, n2, wu16, wd16, tm)
        return (y.reshape(b, t, D_MODEL), k32.reshape(1, b, t, H_ATT, DV_ATT),
                v32.reshape(1, b, t, H_ATT, DV_ATT), s_new[None])

    tq = ATT_BLOCK
    ri = jnp.arange(tq)
    bias_diag = _shifted_bias(rel_bias, ri[None, :] - ri[:, None])
    bias_sub = _shifted_bias(rel_bias, ri[None, :] + tq - ri[:, None])
    r3p = lambda a: a.reshape(bp, sp, a.shape[-1])
    attend_p = lambda qa, k16, vt: _attn_prompt(lam, r3p(qa), r3p(k16), vt, bias_diag, bias_sub, sw_row, tq, lam_init)
    s0p = jnp.zeros((bp, H_REC, DK_REC, DV_REC), F32)
    hgrn_p = lambda q, lf, vr, g: _hgrn(r3p(q), r3p(lf), r3p(vr), r3p(g), s0p, hw_row, 256, HGRN_CHUNK, True)
    y_p, k_p, v_p, s_p = layer(x_prompt, attend_p, hgrn_p, 512, tq)

    half = H_ATT * 2 * ts
    col_h = jnp.arange(half) // (2 * ts)
    col_t = jnp.arange(half) % ts
    assert 2 * half == V7X_LANES and n_pages % 2 == 0

    def own_head(bias):
        out = jnp.zeros(bias.shape[1:], F32)
        for h in range(H_ATT):
            out = jnp.where((col_h == h)[None, :], bias[h], out)
        return out

    keys = jnp.arange(PAGE_SIZE)
    tail_dist = (past + col_t)[None, :] - ((n_pages - 1) * PAGE_SIZE + keys)[:, None]
    zeros_half = jnp.zeros((PAGE_SIZE, half), F32)
    bias_tail = jnp.stack([jnp.zeros((PAGE_SIZE, 2 * half), F32),
                           jnp.concatenate([zeros_half, own_head(_shifted_bias(rel_bias, tail_dist))], axis=1)])
    new_dist = jnp.where((keys < ts)[:, None], col_t[None, :] - keys[:, None], -1)
    bias_new = jnp.concatenate([own_head(_shifted_bias(rel_bias, new_dist)), zeros_half], axis=1)
    ck = cache_k[l].reshape(cache_k.shape[1], PAGE_SIZE * H_ATT, DV_ATT)
    cv = cache_v[l].reshape(cache_v.shape[1], PAGE_SIZE * H_ATT, DV_ATT)

    def attend_s(qa, k16, vt):
        q4 = qa.reshape(bs, ts, H_ATT, DV_ATT).transpose(0, 2, 3, 1)
        d = jnp.arange(DV_ATT)[:, None]
        qcols = jnp.concatenate([jnp.where(d < DK_ATT, q4, 0), jnp.where(d >= DK_ATT, q4, 0)], axis=3)
        eye = jnp.eye(H_ATT, dtype=qcols.dtype)
        wstack = (qcols[:, :, :, None, :] * eye[None, :, None, :, None]).reshape(bs, W_ATT, half)
        zero = jnp.zeros_like(wstack)
        wpair = jnp.concatenate([jnp.concatenate([wstack, zero], axis=2),
                                 jnp.concatenate([zero, wstack], axis=2)], axis=1).astype(BF16)
        return _attn_sample(page_table, lam, wpair, k16.reshape(bs, ts, W_ATT), vt.reshape(bs, ts, W_ATT),
                            bias_new, bias_tail, sw_row, ck, cv, ts, lam_init)

    r3s = lambda a: a.reshape(1, bs * ts, a.shape[-1])
    hgrn_rows = 16 * ts
    hgrn_s = lambda q, lf, vr, g: _hgrn(r3s(q), r3s(lf), r3s(vr), r3s(g), state_hgrn[l].astype(F32), hw_row,
                                        hgrn_rows, ts, False)
    y_s, k_s, v_s, s_s = layer(x_sample, attend_s, hgrn_s, 512, None)
    return (y_p, y_s, k_p, v_p, s_p, k_s, v_s, s_s)
```

```python
import functools
import math

import jax
import jax.numpy as jnp
from jax import lax
from jax.experimental import pallas as pl
from jax.experimental.pallas import tpu as pltpu

F32 = jnp.float32
BF16 = jnp.bfloat16

D_MODEL = 1024
H_REC, DK_REC, DV_REC = 4, 128, 128
W_REC = H_REC * DK_REC
H_ATT, DK_ATT = 4, 64
DV_ATT = 2 * DK_ATT
W_ATT = H_ATT * DV_ATT
D_FF = 4 * D_MODEL
NUM_BUCKETS = 32
MAX_DISTANCE = 128
PAGE_SIZE = 128
HGRN_CHUNK = 32
EPS = 1e-6
NEG = -1e30
LOG2E = math.log2(math.e)

OFF_HQ, OFF_HF, OFF_HI, OFF_HG = 0, W_REC, 2 * W_REC, 3 * W_REC
OFF_AQ = 4 * W_REC
OFF_AK = OFF_AQ + W_ATT
OFF_AV = OFF_AK + W_ATT
OFF_GA = OFF_AV + W_ATT
OFF_GB = OFF_GA + D_MODEL
IN_WIDTH = OFF_GB + D_MODEL

V7X_VMEM_LIMIT_BYTES = 56 * 1024 * 1024
V7X_LANES = 128
PAGES_PER_STEP = 16
PAGE_SLOTS = 2
ATT_BLOCK = 256
ATT_TILE = 128
ATT_HEAD_GROUP = 4

_NT = (((1,), (1,)), ((), ()))


def _const_spec(shape):
    n = len(shape)
    return pl.BlockSpec(shape, lambda *_: (0,) * n, pipeline_mode=pl.Buffered(1))


def _sigmoid(x):
    return 1.0 / (1.0 + jnp.exp(-x))


def _in_proj_kernel(x_ref, n1_ref, w_ref, lb_ref, qn_ref, kn_ref, pseg_ref,
                    q_ref, lf_ref, vr_ref, g_ref, qa_ref, k32_ref, k16_ref, v32_ref, vt_ref,
                    ga_ref, gb_ref):
    tm = x_ref.shape[0]
    x = x_ref[...]
    ms = jnp.mean(x * x, axis=-1, keepdims=True)
    xb = (x * lax.rsqrt(ms + EPS) * n1_ref[...]).astype(BF16)

    def proj(off, size):
        return jnp.dot(xb, w_ref[:, off:off + size], preferred_element_type=F32)

    def seg_rms(z):
        m = jnp.dot((z * z).astype(BF16), pseg_ref[...], preferred_element_type=F32)
        return z * lax.rsqrt(m + EPS)

    def store_head_rows(ref, z):
        for h in range(H_ATT):
            ref[pl.ds(h, tm, stride=H_ATT), :] = z[:, h * DV_ATT:(h + 1) * DV_ATT]

    hq = proj(OFF_HQ, W_REC)
    q_ref[...] = (hq * _sigmoid(hq)).astype(BF16)
    hf = proj(OFF_HF, W_REC)
    lb = lb_ref[...]
    lf_ref[...] = jnp.log(lb + (1.0 - lb) * _sigmoid(hf))
    vr_ref[...] = proj(OFF_HI, W_REC).astype(BF16)
    hg = proj(OFF_HG, W_REC)
    g_ref[...] = (hg * _sigmoid(hg)).astype(BF16)

    qa_ref[...] = (seg_rms(proj(OFF_AQ, W_ATT)) * qn_ref[...]).astype(BF16)
    ka = seg_rms(proj(OFF_AK, W_ATT)) * kn_ref[...]
    store_head_rows(k32_ref, ka)
    k16_ref[...] = ka.astype(BF16)
    va = proj(OFF_AV, W_ATT)
    store_head_rows(v32_ref, va)
    if len(vt_ref.shape) == 2:
        vt_ref[...] = va.astype(BF16)
    else:
        kb = vt_ref.shape[2]
        for i in range(vt_ref.shape[0]):
            vt_ref[i] = va[i * kb:(i + 1) * kb, :].T.astype(BF16)
    ga_ref[...] = _sigmoid(proj(OFF_GA, D_MODEL)).astype(BF16)
    gb_ref[...] = _sigmoid(proj(OFF_GB, D_MODEL)).astype(BF16)


def _in_proj(x2d, n1, w_in16, lb, qn_row, kn_row, pseg, tm, kb):
    m = x2d.shape[0]
    row = lambda w: pl.BlockSpec((tm, w), lambda i: (i, 0))
    head_rows = pl.BlockSpec((tm * H_ATT, DV_ATT), lambda i: (i, 0))
    if kb is None:
        vt_spec, vt_shape = row(W_ATT), (m, W_ATT)
    else:
        vt_spec, vt_shape = pl.BlockSpec((tm // kb, W_ATT, kb), lambda i: (i, 0, 0)), (m // kb, W_ATT, kb)
    sds = jax.ShapeDtypeStruct
    return pl.pallas_call(
        _in_proj_kernel,
        grid=(m // tm,),
        in_specs=[row(D_MODEL), _const_spec((1, D_MODEL)), _const_spec((D_MODEL, IN_WIDTH)),
                  _const_spec((1, W_REC)), _const_spec((1, W_ATT)), _const_spec((1, W_ATT)),
                  _const_spec((W_ATT, W_ATT))],
        out_specs=[row(W_REC), row(W_REC), row(W_REC), row(W_REC), row(W_ATT), head_rows, row(W_ATT),
                   head_rows, vt_spec, row(D_MODEL), row(D_MODEL)],
        out_shape=[sds((m, W_REC), BF16), sds((m, W_REC), F32), sds((m, W_REC), BF16), sds((m, W_REC), BF16),
                   sds((m, W_ATT), BF16), sds((m * H_ATT, DV_ATT), F32), sds((m, W_ATT), BF16),
                   sds((m * H_ATT, DV_ATT), F32), sds(vt_shape, BF16),
                   sds((m, D_MODEL), BF16), sds((m, D_MODEL), BF16)],
        compiler_params=pltpu.CompilerParams(dimension_semantics=("parallel",),
                                             vmem_limit_bytes=V7X_VMEM_LIMIT_BYTES),
        name="in_proj",
    )(x2d, n1, w_in16, lb, qn_row, kn_row, pseg)


def _hgrn_kernel(q_ref, lf_ref, v_ref, g_ref, s0_ref, w_ref, y_ref, s_ref, st_scr, *, c, carry):
    r = q_ref.shape[1]
    nc = r // c
    t = pl.program_id(1)
    mm = BF16 if c >= 16 else F32
    shift = c.bit_length() - 1
    r_i = lax.broadcasted_iota(jnp.int32, (r, r), 0)
    c_i = lax.broadcasted_iota(jnp.int32, (r, r), 1)
    tril = ((r_i >> shift) == (c_i >> shift)) & (r_i >= c_i)
    tril16 = jnp.where(tril, 1.0, 0.0).astype(BF16)
    lane_chunk = lax.broadcasted_iota(jnp.int32, (DV_REC, r), 1) >> shift

    if carry:
        @pl.when(t == 0)
        def _():
            for h in range(H_REC):
                st_scr[h] = s0_ref[0, h].T

    for h in range(H_REC):
        cols = slice(h * DK_REC, (h + 1) * DK_REC)
        lf = lf_ref[0, :, cols]
        hi = lf.astype(BF16)
        lo = (lf - hi.astype(F32)).astype(BF16)
        b = jnp.dot(tril16, hi, preferred_element_type=F32) + jnp.dot(tril16, lo, preferred_element_type=F32)
        b3 = b.reshape(nc, c, DK_REC)
        bl = b3[:, c - 1:c, :]
        k = 1.0 - jnp.exp(lf)
        q = q_ref[0, :, cols].astype(F32)
        v16 = v_ref[0, :, cols]
        qe = q * jnp.exp(b)
        qe16 = qe.astype(BF16)
        ke16 = (k * jnp.exp(-b)).astype(BF16)
        kd16 = (k.reshape(nc, c, DK_REC) * jnp.exp(bl - b3)).reshape(r, DK_REC).astype(BF16)
        dec = jnp.exp(bl)
        a = lax.dot_general(qe16, ke16, _NT, preferred_element_type=F32)
        o = jnp.dot(jnp.where(tril, a, 0.0).astype(BF16), v16, preferred_element_type=F32)
        vt = v16.astype(F32).T
        lhs = jnp.concatenate([jnp.where(lane_chunk == n, vt, 0.0) for n in range(nc)], axis=0).astype(BF16)
        u = jnp.dot(lhs, kd16, preferred_element_type=F32)
        qe_mm = qe16 if mm == BF16 else qe
        parts = []
        if carry:
            st = st_scr[h]
            for n in range(nc):
                parts.append(lax.dot_general(qe_mm[n * c:(n + 1) * c], st.astype(mm), _NT,
                                             preferred_element_type=F32))
                st = st * dec[n] + u[n * DV_REC:(n + 1) * DV_REC]
            st_scr[h] = st
        else:
            for n in range(nc):
                st = s0_ref[n, h].T
                parts.append(lax.dot_general(qe_mm[n * c:(n + 1) * c], st.astype(mm), _NT,
                                             preferred_element_type=F32))
                s_ref[n, h] = (st * dec[n] + u[n * DV_REC:(n + 1) * DV_REC]).T
        o = o + jnp.concatenate(parts, axis=0)
        on = o * lax.rsqrt(jnp.mean(o * o, axis=-1, keepdims=True) + EPS) * w_ref[...]
        y_ref[0, :, cols] = (on * g_ref[0, :, cols].astype(F32)).astype(BF16)

    if carry:
        @pl.when(t == pl.num_programs(1) - 1)
        def _():
            for h in range(H_REC):
                s_ref[0, h] = st_scr[h].T


def _hgrn(q, lf, v, g, s0, w_row, rows, c, carry):
    gdim, t, _ = q.shape
    nseq = 1 if carry else rows // c
    blk = pl.BlockSpec((1, rows, W_REC), lambda gi, ti: (gi, ti, 0))
    if carry:
        sblk = pl.BlockSpec((1, H_REC, DK_REC, DV_REC), lambda gi, ti: (gi, 0, 0, 0))
    else:
        sblk = pl.BlockSpec((nseq, H_REC, DK_REC, DV_REC), lambda gi, ti: (ti, 0, 0, 0))
    return pl.pallas_call(
        functools.partial(_hgrn_kernel, c=c, carry=carry),
        grid=(gdim, t // rows),
        in_specs=[blk, blk, blk, blk, sblk, _const_spec((1, DV_REC))],
        out_specs=[blk, sblk],
        out_shape=[jax.ShapeDtypeStruct((gdim, t, W_REC), BF16), jax.ShapeDtypeStruct(s0.shape, F32)],
        scratch_shapes=[pltpu.VMEM((H_REC, DV_REC, DK_REC), F32)],
        compiler_params=pltpu.CompilerParams(dimension_semantics=("parallel", "arbitrary"),
                                             vmem_limit_bytes=V7X_VMEM_LIMIT_BYTES),
        name="hgrn",
    )(q, lf, v, g, s0, w_row)


def _rel_bucket(dist):
    dist = jnp.maximum(dist, 0)
    max_exact = NUM_BUCKETS // 2
    large = max_exact + (jnp.log(jnp.maximum(dist, 1).astype(F32) / max_exact)
                         / math.log(MAX_DISTANCE / max_exact) * (NUM_BUCKETS - max_exact)).astype(jnp.int32)
    large = jnp.minimum(large, NUM_BUCKETS - 1)
    return jnp.where(dist < max_exact, dist, large)


def _shifted_bias(rel_bias, dist):
    rb = rel_bias.astype(F32)
    rb = (rb - rb[NUM_BUCKETS - 1]) * LOG2E
    bucket = _rel_bucket(dist)[None]
    expand = (slice(None),) + (None,) * dist.ndim
    out = jnp.zeros((rb.shape[1],) + dist.shape, F32)
    for kb in range(NUM_BUCKETS - 1):
        out = jnp.where(bucket == kb, rb[kb][expand], out)
    return jnp.where((dist >= 0)[None], out, NEG)


def _split_halves(q):
    lane = lax.broadcasted_iota(jnp.int32, q.shape, 1)
    zero = jnp.zeros_like(q)
    return jnp.concatenate([jnp.where(lane < DK_ATT, q, zero), jnp.where(lane >= DK_ATT, q, zero)], axis=0)


def _attn_prompt_kernel(lam_ref, q_ref, k_ref, vt_ref, bd_ref, bs_ref, w_ref, o_ref, *, tq, lam_init):
    qi = pl.program_id(1)
    for h0 in range(0, H_ATT, ATT_HEAD_GROUP):
        _attn_prompt_heads(range(h0, h0 + ATT_HEAD_GROUP), qi, lam_ref, q_ref, k_ref, vt_ref, bd_ref, bs_ref,
                           w_ref, o_ref, tq, lam_init)


def _attn_prompt_heads(heads, qi, lam_ref, q_ref, k_ref, vt_ref, bd_ref, bs_ref, w_ref, o_ref, tq, lam_init):
    nt = 2 * tq // ATT_TILE
    cols = {h: slice(h * DV_ATT, (h + 1) * DV_ATT) for h in heads}
    q_tiles = {}
    for h in heads:
        qs = _split_halves(q_ref[0, :, cols[h]])
        q_tiles[h] = [qs[i * ATT_TILE:(i + 1) * ATT_TILE] for i in range(nt)]

    def scores(j):
        out = []
        for h in heads:
            k_blk = k_ref[0, pl.ds(pl.multiple_of(j * tq, tq), tq), cols[h]]
            out.append(tuple(lax.dot_general(k_blk, q_tiles[h][i], _NT, preferred_element_type=F32)
                             for i in range(nt)))
        return tuple(out)

    def update(j, s_all, bias_ref, extra, carry):
        out = []
        for g, h in enumerate(heads):
            vt_blk = vt_ref[j, cols[h], :]
            tiles = []
            for i in range(nt):
                m, l, acc = carry[g][i]
                s = s_all[g][i]
                if bias_ref is not None:
                    c0 = (i * ATT_TILE) % tq
                    s = s + bias_ref[h, :, c0:c0 + ATT_TILE]
                    if extra is not None:
                        s = s + extra
                m_new = jnp.maximum(m, jnp.max(s, axis=0, keepdims=True))
                alpha = jnp.exp2(m - m_new)
                p = jnp.exp2(s - m_new)
                l = l * alpha + jnp.sum(p, axis=0, keepdims=True)
                acc = acc * alpha + jnp.dot(vt_blk, p.astype(BF16), preferred_element_type=F32)
                tiles.append((m_new, l, acc))
            out.append(tuple(tiles))
        return tuple(out)

    init = tuple(tuple((jnp.full((1, ATT_TILE), NEG, F32), jnp.zeros((1, ATT_TILE), F32),
                        jnp.zeros((DV_ATT, ATT_TILE), F32)) for _ in range(nt)) for _ in heads)
    sub_j = jnp.maximum(qi - 1, 0)
    n_far = jnp.maximum(qi - 1, 0)
    s_diag = scores(qi)
    s_sub = scores(sub_j)
    carry = update(qi, s_diag, bd_ref, None, init)
    carry = update(sub_j, s_sub, bs_ref, jnp.where(qi == 0, NEG, 0.0), carry)

    def far_step(j, cr):
        return update(j, scores(j), None, None, cr)

    carry = lax.fori_loop(0, n_far, far_step, carry)
    for g, h in enumerate(heads):
        o_t = jnp.concatenate([acc / l for _, l, acc in carry[g]], axis=1)
        o_t = o_t[:, :tq] - lam_ref[0] * o_t[:, tq:]
        o_t = o_t * lax.rsqrt(jnp.mean(o_t * o_t, axis=0, keepdims=True) + EPS)
        o_ref[0, :, cols[h]] = (o_t.T * w_ref[...] * (1.0 - lam_init)).astype(BF16)


def _attn_prompt(lam, qa, k16, vt, bias_diag, bias_sub, w_row, tq, lam_init):
    b, s, _ = qa.shape
    nb = s // tq
    assert tq >= MAX_DISTANCE and s % tq == 0
    return pl.pallas_call(
        functools.partial(_attn_prompt_kernel, tq=tq, lam_init=lam_init),
        grid=(b, nb),
        in_specs=[pl.BlockSpec(memory_space=pltpu.SMEM),
                  pl.BlockSpec((1, tq, W_ATT), lambda bi, qi: (bi, qi, 0)),
                  pl.BlockSpec((1, s, W_ATT), lambda bi, qi: (bi, 0, 0)),
                  pl.BlockSpec((nb, W_ATT, tq), lambda bi, qi: (bi, 0, 0)),
                  _const_spec((H_ATT, tq, tq)), _const_spec((H_ATT, tq, tq)), _const_spec((1, DV_ATT))],
        out_specs=pl.BlockSpec((1, tq, W_ATT), lambda bi, qi: (bi, qi, 0)),
        out_shape=jax.ShapeDtypeStruct((b, s, W_ATT), BF16),
        compiler_params=pltpu.CompilerParams(dimension_semantics=("parallel", "arbitrary")),
        name="attn_prompt",
    )(lam, qa, k16, vt, bias_diag, bias_sub, w_row)


def _attn_sample_kernel(pt_ref, lam_ref, wp_ref, kn_ref, vn_ref, bnew_ref, btail_ref, w_ref, ck_hbm, cv_hbm,
                        o_ref, kbuf, vbuf, sem, m_scr, l_scr, acc_scr, *, t_new, lam_init):
    npg = PAGES_PER_STEP
    j = pl.program_id(1)
    n_steps = pl.num_programs(1)
    g = pl.program_id(0) * n_steps + j
    last = pl.num_programs(0) * n_steps - 1
    slot = lax.rem(g, 2)
    rh = 2 * t_new
    half = H_ATT * rh
    ncol = 2 * half
    wpair = wp_ref[0]

    def page_copies(step, slot_):
        row, col0 = lax.div(step, n_steps), lax.rem(step, n_steps) * npg
        copies = []
        for i in range(npg):
            page = pt_ref[row, col0 + i]
            copies.append(pltpu.make_async_copy(ck_hbm.at[page], kbuf.at[slot_, i], sem.at[slot_]))
            copies.append(pltpu.make_async_copy(cv_hbm.at[page], vbuf.at[slot_, i], sem.at[slot_]))
        return copies

    @pl.when(g == 0)
    def _():
        for c in page_copies(g, slot):
            c.start()

    for c in page_copies(g, slot):
        c.wait()
    k_refs = [kbuf.at[slot, i] for i in range(npg)]
    v_refs = [vbuf.at[slot, i] for i in range(npg)]

    def head_rows(ref, h):
        return ref[pl.ds(h, PAGE_SIZE, stride=H_ATT), :].astype(BF16)

    def rows_to_cols(row):
        return jnp.broadcast_to(row, (row.shape[1], row.shape[1])).T

    def weighted_values(p_t, get_v, parity):
        return [jnp.dot(p_t[parity * half + h * rh:parity * half + (h + 1) * rh], get_v(h),
                        preferred_element_type=F32) for h in range(H_ATT)]

    @pl.when(j == 0)
    def _():
        pad = PAGE_SIZE - t_new
        k_new = jnp.concatenate([kn_ref[0], jnp.zeros((pad, W_ATT), BF16)], axis=0)
        v_new = jnp.concatenate([vn_ref[0], jnp.zeros((pad, W_ATT), BF16)], axis=0)
        s = jnp.dot(k_new, wpair[:W_ATT, :], preferred_element_type=F32) + bnew_ref[...]
        m = jnp.max(s, axis=0, keepdims=True)
        even = lax.broadcasted_iota(jnp.int32, s.shape, 1) < half
        p = jnp.where(even, jnp.exp2(s - m), 0.0)
        m_scr[...] = jnp.where(even[:1], m, NEG)
        l_scr[...] = jnp.sum(p, axis=0, keepdims=True)
        p_t = p.T.astype(BF16)
        acc_scr[...] = jnp.concatenate(
            weighted_values(p_t, lambda h: v_new[:, h * DV_ATT:(h + 1) * DV_ATT], 0)
            + [jnp.zeros((half, DV_ATT), F32)], axis=0)

    nxt = jnp.minimum(g + 1, last)
    for c in page_copies(nxt, 1 - slot):
        c.start()
    tiles = []
    for pi in range(npg // 2):
        lhs = jnp.concatenate([head_rows(k_refs[2 * pi + par], h) for par in range(2) for h in range(H_ATT)],
                              axis=1)
        s = jnp.dot(lhs, wpair, preferred_element_type=F32)
        if pi == npg // 2 - 1:
            s = s + btail_ref[0]
        tiles.append(s)
    mx = tiles[0]
    for s in tiles[1:]:
        mx = jnp.maximum(mx, s)
    m_old = m_scr[...]
    m_new = jnp.maximum(m_old, jnp.max(mx, axis=0, keepdims=True))
    alpha = jnp.exp2(m_old - m_new)
    acc = acc_scr[...] * rows_to_cols(alpha)
    lsum = jnp.zeros_like(alpha)
    for pi in range(npg // 2):
        p = jnp.exp2(tiles[pi] - m_new)
        lsum = lsum + jnp.sum(p, axis=0, keepdims=True)
        p_t = p.T.astype(BF16)
        acc = acc + jnp.concatenate(
            [pv for par in range(2)
             for pv in weighted_values(p_t, functools.partial(head_rows, v_refs[2 * pi + par]), par)], axis=0)
    l = l_scr[...] * alpha + lsum
    m_scr[...] = m_new
    l_scr[...] = l
    acc_scr[...] = acc

    @pl.when(j == pl.num_programs(1) - 1)
    def _():
        m_c, l_c = rows_to_cols(m_new), rows_to_cols(l)
        m_f = jnp.maximum(m_c[:half], m_c[half:])
        w_e, w_o = jnp.exp2(m_c[:half] - m_f), jnp.exp2(m_c[half:] - m_f)
        o = (acc[:half] * w_e + acc[half:] * w_o) / (l_c[:half] * w_e + l_c[half:] * w_o)
        for h in range(H_ATT):
            oh = o[h * rh:h * rh + t_new] - lam_ref[0] * o[h * rh + t_new:(h + 1) * rh]
            oh = oh * lax.rsqrt(jnp.mean(oh * oh, axis=-1, keepdims=True) + EPS) * w_ref[...] * (1.0 - lam_init)
            o_ref[0, :, h * DV_ATT:(h + 1) * DV_ATT] = oh.astype(BF16)

    @pl.when(g == last)
    def _():
        for c in page_copies(nxt, 1 - slot):
            c.wait()


def _attn_sample(page_table, lam, wpair, k_new, v_new, bias_new, bias_tail, w_row, cache_k, cache_v,
                 t_new, lam_init):
    nb, n_pages = page_table.shape
    npg = PAGES_PER_STEP
    assert n_pages % npg == 0 and npg % 2 == 0
    n_steps = n_pages // npg
    ncol = wpair.shape[2]
    prow = PAGE_SIZE * H_ATT
    per_b = lambda n, w: pl.BlockSpec((1, n, w), lambda b, j, pt: (b, 0, 0))
    const = lambda shape: pl.BlockSpec(shape, lambda b, j, pt: (0,) * len(shape))
    hbm = pl.BlockSpec(memory_space=pl.ANY)
    grid_spec = pltpu.PrefetchScalarGridSpec(
        num_scalar_prefetch=1,
        grid=(nb, n_steps),
        in_specs=[pl.BlockSpec(memory_space=pltpu.SMEM), per_b(2 * W_ATT, ncol), per_b(t_new, W_ATT),
                  per_b(t_new, W_ATT), const((PAGE_SIZE, ncol)),
                  pl.BlockSpec((1, PAGE_SIZE, ncol), lambda b, j, pt: (jnp.where(j == n_steps - 1, 1, 0), 0, 0)),
                  const((1, DV_ATT)), hbm, hbm],
        out_specs=pl.BlockSpec((1, t_new, W_ATT), lambda b, j, pt: (b, 0, 0)),
        scratch_shapes=[pltpu.VMEM((PAGE_SLOTS, npg, prow, DV_ATT), F32),
                        pltpu.VMEM((PAGE_SLOTS, npg, prow, DV_ATT), F32),
                        pltpu.SemaphoreType.DMA((PAGE_SLOTS,)),
                        pltpu.VMEM((1, ncol), F32), pltpu.VMEM((1, ncol), F32), pltpu.VMEM((ncol, DV_ATT), F32)],
    )
    return pl.pallas_call(
        functools.partial(_attn_sample_kernel, t_new=t_new, lam_init=lam_init),
        grid_spec=grid_spec,
        out_shape=jax.ShapeDtypeStruct((nb, t_new, W_ATT), BF16),
        compiler_params=pltpu.CompilerParams(dimension_semantics=("arbitrary", "arbitrary"),
                                             vmem_limit_bytes=V7X_VMEM_LIMIT_BYTES),
        name="attn_sample",
    )(page_table, lam, wpair, k_new, v_new, bias_new, bias_tail, w_row, cache_k, cache_v)


def _out_mlp_kernel(x_ref, yr_ref, ya_ref, ga_ref, gb_ref, wa_ref, wb_ref, wo_ref, n2_ref, wu_ref, wd_ref,
                    y_ref, *, ff_chunk):
    a = jnp.dot(yr_ref[...], wa_ref[...], preferred_element_type=F32)
    b = jnp.dot(ya_ref[...], wb_ref[...], preferred_element_type=F32)
    mixed = ga_ref[...].astype(F32) * a + gb_ref[...].astype(F32) * b
    h = x_ref[...] + jnp.dot(mixed.astype(BF16), wo_ref[...], preferred_element_type=F32)
    ms = jnp.mean(h * h, axis=-1, keepdims=True)
    hn = (h * lax.rsqrt(ms + EPS) * n2_ref[...]).astype(BF16)
    y = h
    for c0 in range(0, D_FF, ff_chunk):
        u = jnp.maximum(jnp.dot(hn, wu_ref[:, c0:c0 + ff_chunk], preferred_element_type=F32), 0.0)
        y = y + jnp.dot((u * u).astype(BF16), wd_ref[c0:c0 + ff_chunk, :], preferred_element_type=F32)
    y_ref[...] = y


def _out_mlp(x2d, y_rec, y_att, ga, gb, wa16, wb16, wo16, n2, wu16, wd16, tm):
    m = x2d.shape[0]
    row = lambda w: pl.BlockSpec((tm, w), lambda i: (i, 0))
    return pl.pallas_call(
        functools.partial(_out_mlp_kernel, ff_chunk=D_MODEL),
        grid=(m // tm,),
        in_specs=[row(D_MODEL), row(W_REC), row(W_ATT), row(D_MODEL), row(D_MODEL),
                  _const_spec((W_REC, D_MODEL)), _const_spec((W_ATT, D_MODEL)), _const_spec((D_MODEL, D_MODEL)),
                  _const_spec((1, D_MODEL)), _const_spec((D_MODEL, D_FF)), _const_spec((D_FF, D_MODEL))],
        out_specs=row(D_MODEL),
        out_shape=jax.ShapeDtypeStruct((m, D_MODEL), F32),
        compiler_params=pltpu.CompilerParams(dimension_semantics=("parallel",),
                                             vmem_limit_bytes=V7X_VMEM_LIMIT_BYTES),
        name="out_mlp",
    )(x2d, y_rec, y_att, ga, gb, wa16, wb16, wo16, n2, wu16, wd16)


def kernel(x_prompt, x_sample, cache_k, cache_v, state_hgrn, page_table, norm1_w, w_in, lower_bounds, hgrn_norm_w, q_norm_w, k_norm_w, lam_q1, lam_k1, lam_q2, lam_k2, subln_w, rel_bias, w_branch_a, w_branch_b, w_out, norm2_w, w_up, w_down):
    depth = w_in.shape[0]
    assert depth == 1
    bp, sp, _ = x_prompt.shape
    bs, ts, _ = x_sample.shape
    n_pages = page_table.shape[1]
    past = n_pages * PAGE_SIZE
    assert PAGE_SIZE >= MAX_DISTANCE - 1 and ts <= HGRN_CHUNK
    l = 0
    lam_init = 0.8 - 0.6 * math.exp(-0.3 * l)
    scale = DK_ATT ** -0.5

    lb = jnp.cumsum(jax.nn.softmax(lower_bounds.astype(F32), axis=0), axis=0)[l][None, :]
    lam = (jnp.exp(jnp.dot(lam_q1[l].astype(F32), lam_k1[l].astype(F32)))
           - jnp.exp(jnp.dot(lam_q2[l].astype(F32), lam_k2[l].astype(F32))) + lam_init).reshape(1)
    n1 = norm1_w[l].astype(F32)[None, :]
    n2 = norm2_w[l].astype(F32)[None, :]
    qn_row = jnp.tile(q_norm_w[l].astype(F32).reshape(-1), H_ATT)[None, :] * (scale * LOG2E)
    kn_row = jnp.tile(k_norm_w[l].astype(F32).reshape(-1), H_ATT)[None, :]
    hw_row = hgrn_norm_w[l].astype(F32)[None, :]
    sw_row = subln_w[l].astype(F32)[None, :]
    seg = jnp.arange(W_ATT) // DK_ATT
    pseg = jnp.where(seg[:, None] == seg[None, :], 1.0 / DK_ATT, 0.0).astype(BF16)
    w_in16, wa16, wb16 = w_in[l].astype(BF16), w_branch_a[l].astype(BF16), w_branch_b[l].astype(BF16)
    wo16, wu16, wd16 = w_out[l].astype(BF16), w_up[l].astype(BF16), w_down[l].astype(BF16)

    def layer(x, attend, hgrn_fn, tm, kb):
        b, t, _ = x.shape
        x2d = x.reshape(b * t, D_MODEL)
        q, lf, vr, g, qa, k32, k16, v32, vt, ga, gb = _in_proj(x2d, n1, w_in16, lb, qn_row, kn_row, pseg, tm, kb)
        y_rec, s_new = hgrn_fn(q, lf, vr, g)
        y_att = attend(qa, k16, vt)
        y = _out_mlp(x2d, y_rec.reshape(b * t, W_REC), y_att.reshape(b * t, W_ATT), ga, gb,
                     wa16, wb16, wo16, n2, wu16, wd16, tm)
        return (y.reshape(b, t, D_MODEL), k32.reshape(1, b, t, H_ATT, DV_ATT),
                v32.reshape(1, b, t, H_ATT, DV_ATT), s_new[None])

    tq = ATT_BLOCK
    ri = jnp.arange(tq)
    bias_diag = _shifted_bias(rel_bias, ri[None, :] - ri[:, None])
    bias_sub = _shifted_bias(rel_bias, ri[None, :] + tq - ri[:, None])
    r3p = lambda a: a.reshape(bp, sp, a.shape[-1])
    attend_p = lambda qa, k16, vt: _attn_prompt(lam, r3p(qa), r3p(k16), vt, bias_diag, bias_sub, sw_row, tq, lam_init)
    s0p = jnp.zeros((bp, H_REC, DK_REC, DV_REC), F32)
    hgrn_p = lambda q, lf, vr, g: _hgrn(r3p(q), r3p(lf), r3p(vr), r3p(g), s0p, hw_row, 256, HGRN_CHUNK, True)
    y_p, k_p, v_p, s_p = layer(x_prompt, attend_p, hgrn_p, 512, tq)

    half = H_ATT * 2 * ts
    col_h = jnp.arange(half) // (2 * ts)
    col_t = jnp.arange(half) % ts
    assert 2 * half == V7X_LANES and n_pages % 2 == 0

    def own_head(bias):
        out = jnp.zeros(bias.shape[1:], F32)
        for h in range(H_ATT):
            out = jnp.where((col_h == h)[None, :], bias[h], out)
        return out

    keys = jnp.arange(PAGE_SIZE)
    tail_dist = (past + col_t)[None, :] - ((n_pages - 1) * PAGE_SIZE + keys)[:, None]
    zeros_half = jnp.zeros((PAGE_SIZE, half), F32)
    bias_tail = jnp.stack([jnp.zeros((PAGE_SIZE, 2 * half), F32),
                           jnp.concatenate([zeros_half, own_head(_shifted_bias(rel_bias, tail_dist))], axis=1)])
    new_dist = jnp.where((keys < ts)[:, None], col_t[None, :] - keys[:, None], -1)
    bias_new = jnp.concatenate([own_head(_shifted_bias(rel_bias, new_dist)), zeros_half], axis=1)
    ck = cache_k[l].reshape(cache_k.shape[1], PAGE_SIZE * H_ATT, DV_ATT)
    cv = cache_v[l].reshape(cache_v.shape[1], PAGE_SIZE * H_ATT, DV_ATT)

    def attend_s(qa, k16, vt):
        q4 = qa.reshape(bs, ts, H_ATT, DV_ATT).transpose(0, 2, 3, 1)
        d = jnp.arange(DV_ATT)[:, None]
        qcols = jnp.concatenate([jnp.where(d < DK_ATT, q4, 0), jnp.where(d >= DK_ATT, q4, 0)], axis=3)
        eye = jnp.eye(H_ATT, dtype=qcols.dtype)
        wstack = (qcols[:, :, :, None, :] * eye[None, :, None, :, None]).reshape(bs, W_ATT, half)
        zero = jnp.zeros_like(wstack)
        wpair = jnp.concatenate([jnp.concatenate([wstack, zero], axis=2),
                                 jnp.concatenate([zero, wstack], axis=2)], axis=1).astype(BF16)
        return _attn_sample(page_table, lam, wpair, k16.reshape(bs, ts, W_ATT), vt.reshape(bs, ts, W_ATT),
                            bias_new, bias_tail, sw_row, ck, cv, ts, lam_init)

    r3s = lambda a: a.reshape(1, bs * ts, a.shape[-1])
    hgrn_rows = 16 * ts
    hgrn_s = lambda q, lf, vr, g: _hgrn(r3s(q), r3s(lf), r3s(vr), r3s(g), state_hgrn[l].astype(F32), hw_row,
                                        hgrn_rows, ts, False)
    y_s, k_s, v_s, s_s = layer(x_sample, attend_s, hgrn_s, 512, None)
    return (y_p, y_s, k_p, v_p, s_p, k_s, v_s, s_s)
```

```python
import functools
import math

import jax
import jax.numpy as jnp
from jax import lax
from jax.experimental import pallas as pl
from jax.experimental.pallas import tpu as pltpu

F32 = jnp.float32
BF16 = jnp.bfloat16

D_MODEL = 1024
H_REC, DK_REC, DV_REC = 4, 128, 128
W_REC = H_REC * DK_REC
H_ATT, DK_ATT = 4, 64
DV_ATT = 2 * DK_ATT
W_ATT = H_ATT * DV_ATT
D_FF = 4 * D_MODEL
NUM_BUCKETS = 32
MAX_DISTANCE = 128
PAGE_SIZE = 128
HGRN_CHUNK = 32
EPS = 1e-6
NEG = -1e30
LOG2E = math.log2(math.e)

OFF_HQ, OFF_HF, OFF_HI, OFF_HG = 0, W_REC, 2 * W_REC, 3 * W_REC
OFF_AQ = 4 * W_REC
OFF_AK = OFF_AQ + W_ATT
OFF_AV = OFF_AK + W_ATT
OFF_GA = OFF_AV + W_ATT
OFF_GB = OFF_GA + D_MODEL
IN_WIDTH = OFF_GB + D_MODEL

V7X_VMEM_LIMIT_BYTES = 56 * 1024 * 1024
V7X_LANES = 128
PAGES_PER_STEP = 16
PAGE_SLOTS = 3
ATT_BLOCK = 256
ATT_TILE = 128
ATT_HEAD_GROUP = 4

_NT = (((1,), (1,)), ((), ()))


def _const_spec(shape):
    n = len(shape)
    return pl.BlockSpec(shape, lambda *_: (0,) * n, pipeline_mode=pl.Buffered(1))


def _sigmoid(x):
    return 1.0 / (1.0 + jnp.exp(-x))


def _in_proj_kernel(x_ref, n1_ref, w_ref, lb_ref, qn_ref, kn_ref, pseg_ref,
                    q_ref, lf_ref, vr_ref, g_ref, qa_ref, k32_ref, k16_ref, v32_ref, vt_ref,
                    ga_ref, gb_ref):
    tm = x_ref.shape[0]
    x = x_ref[...]
    ms = jnp.mean(x * x, axis=-1, keepdims=True)
    xb = (x * lax.rsqrt(ms + EPS) * n1_ref[...]).astype(BF16)

    def proj(off, size):
        return jnp.dot(xb, w_ref[:, off:off + size], preferred_element_type=F32)

    def seg_rms(z):
        m = jnp.dot((z * z).astype(BF16), pseg_ref[...], preferred_element_type=F32)
        return z * lax.rsqrt(m + EPS)

    def store_head_rows(ref, z):
        for h in range(H_ATT):
            ref[pl.ds(h, tm, stride=H_ATT), :] = z[:, h * DV_ATT:(h + 1) * DV_ATT]

    hq = proj(OFF_HQ, W_REC)
    q_ref[...] = (hq * _sigmoid(hq)).astype(BF16)
    hf = proj(OFF_HF, W_REC)
    lb = lb_ref[...]
    lf_ref[...] = jnp.log(lb + (1.0 - lb) * _sigmoid(hf))
    vr_ref[...] = proj(OFF_HI, W_REC).astype(BF16)
    hg = proj(OFF_HG, W_REC)
    g_ref[...] = (hg * _sigmoid(hg)).astype(BF16)

    qa_ref[...] = (seg_rms(proj(OFF_AQ, W_ATT)) * qn_ref[...]).astype(BF16)
    ka = seg_rms(proj(OFF_AK, W_ATT)) * kn_ref[...]
    store_head_rows(k32_ref, ka)
    k16_ref[...] = ka.astype(BF16)
    va = proj(OFF_AV, W_ATT)
    store_head_rows(v32_ref, va)
    if len(vt_ref.shape) == 2:
        vt_ref[...] = va.astype(BF16)
    else:
        kb = vt_ref.shape[2]
        for i in range(vt_ref.shape[0]):
            vt_ref[i] = va[i * kb:(i + 1) * kb, :].T.astype(BF16)
    ga_ref[...] = _sigmoid(proj(OFF_GA, D_MODEL)).astype(BF16)
    gb_ref[...] = _sigmoid(proj(OFF_GB, D_MODEL)).astype(BF16)


def _in_proj(x2d, n1, w_in16, lb, qn_row, kn_row, pseg, tm, kb):
    m = x2d.shape[0]
    row = lambda w: pl.BlockSpec((tm, w), lambda i: (i, 0))
    head_rows = pl.BlockSpec((tm * H_ATT, DV_ATT), lambda i: (i, 0))
    if kb is None:
        vt_spec, vt_shape = row(W_ATT), (m, W_ATT)
    else:
        vt_spec, vt_shape = pl.BlockSpec((tm // kb, W_ATT, kb), lambda i: (i, 0, 0)), (m // kb, W_ATT, kb)
    sds = jax.ShapeDtypeStruct
    return pl.pallas_call(
        _in_proj_kernel,
        grid=(m // tm,),
        in_specs=[row(D_MODEL), _const_spec((1, D_MODEL)), _const_spec((D_MODEL, IN_WIDTH)),
                  _const_spec((1, W_REC)), _const_spec((1, W_ATT)), _const_spec((1, W_ATT)),
                  _const_spec((W_ATT, W_ATT))],
        out_specs=[row(W_REC), row(W_REC), row(W_REC), row(W_REC), row(W_ATT), head_rows, row(W_ATT),
                   head_rows, vt_spec, row(D_MODEL), row(D_MODEL)],
        out_shape=[sds((m, W_REC), BF16), sds((m, W_REC), F32), sds((m, W_REC), BF16), sds((m, W_REC), BF16),
                   sds((m, W_ATT), BF16), sds((m * H_ATT, DV_ATT), F32), sds((m, W_ATT), BF16),
                   sds((m * H_ATT, DV_ATT), F32), sds(vt_shape, BF16),
                   sds((m, D_MODEL), BF16), sds((m, D_MODEL), BF16)],
        compiler_params=pltpu.CompilerParams(dimension_semantics=("parallel",),
                                             vmem_limit_bytes=V7X_VMEM_LIMIT_BYTES),
        name="in_proj",
    )(x2d, n1, w_in16, lb, qn_row, kn_row, pseg)


def _hgrn_kernel(q_ref, lf_ref, v_ref, g_ref, s0_ref, w_ref, y_ref, s_ref, st_scr, *, c, carry):
    r = q_ref.shape[1]
    nc = r // c
    t = pl.program_id(1)
    mm = BF16 if c >= 16 else F32
    shift = c.bit_length() - 1
    r_i = lax.broadcasted_iota(jnp.int32, (r, r), 0)
    c_i = lax.broadcasted_iota(jnp.int32, (r, r), 1)
    tril = ((r_i >> shift) == (c_i >> shift)) & (r_i >= c_i)
    tril16 = jnp.where(tril, 1.0, 0.0).astype(BF16)
    lane_chunk = lax.broadcasted_iota(jnp.int32, (DV_REC, r), 1) >> shift

    if carry:
        @pl.when(t == 0)
        def _():
            for h in range(H_REC):
                st_scr[h] = s0_ref[0, h].T

    for h in range(H_REC):
        cols = slice(h * DK_REC, (h + 1) * DK_REC)
        lf = lf_ref[0, :, cols]
        hi = lf.astype(BF16)
        lo = (lf - hi.astype(F32)).astype(BF16)
        b = jnp.dot(tril16, hi, preferred_element_type=F32) + jnp.dot(tril16, lo, preferred_element_type=F32)
        b3 = b.reshape(nc, c, DK_REC)
        bl = b3[:, c - 1:c, :]
        k = 1.0 - jnp.exp(lf)
        q = q_ref[0, :, cols].astype(F32)
        v16 = v_ref[0, :, cols]
        qe = q * jnp.exp(b)
        qe16 = qe.astype(BF16)
        ke16 = (k * jnp.exp(-b)).astype(BF16)
        kd16 = (k.reshape(nc, c, DK_REC) * jnp.exp(bl - b3)).reshape(r, DK_REC).astype(BF16)
        dec = jnp.exp(bl)
        a = lax.dot_general(qe16, ke16, _NT, preferred_element_type=F32)
        o = jnp.dot(jnp.where(tril, a, 0.0).astype(BF16), v16, preferred_element_type=F32)
        vt = v16.astype(F32).T
        lhs = jnp.concatenate([jnp.where(lane_chunk == n, vt, 0.0) for n in range(nc)], axis=0).astype(BF16)
        u = jnp.dot(lhs, kd16, preferred_element_type=F32)
        qe_mm = qe16 if mm == BF16 else qe
        parts = []
        if carry:
            st = st_scr[h]
            for n in range(nc):
                parts.append(lax.dot_general(qe_mm[n * c:(n + 1) * c], st.astype(mm), _NT,
                                             preferred_element_type=F32))
                st = st * dec[n] + u[n * DV_REC:(n + 1) * DV_REC]
            st_scr[h] = st
        else:
            for n in range(nc):
                st = s0_ref[n, h].T
                parts.append(lax.dot_general(qe_mm[n * c:(n + 1) * c], st.astype(mm), _NT,
                                             preferred_element_type=F32))
                s_ref[n, h] = (st * dec[n] + u[n * DV_REC:(n + 1) * DV_REC]).T
        o = o + jnp.concatenate(parts, axis=0)
        on = o * lax.rsqrt(jnp.mean(o * o, axis=-1, keepdims=True) + EPS) * w_ref[...]
        y_ref[0, :, cols] = (on * g_ref[0, :, cols].astype(F32)).astype(BF16)

    if carry:
        @pl.when(t == pl.num_programs(1) - 1)
        def _():
            for h in range(H_REC):
                s_ref[0, h] = st_scr[h].T


def _hgrn(q, lf, v, g, s0, w_row, rows, c, carry):
    gdim, t, _ = q.shape
    nseq = 1 if carry else rows // c
    blk = pl.BlockSpec((1, rows, W_REC), lambda gi, ti: (gi, ti, 0))
    if carry:
        sblk = pl.BlockSpec((1, H_REC, DK_REC, DV_REC), lambda gi, ti: (gi, 0, 0, 0))
    else:
        sblk = pl.BlockSpec((nseq, H_REC, DK_REC, DV_REC), lambda gi, ti: (ti, 0, 0, 0))
    return pl.pallas_call(
        functools.partial(_hgrn_kernel, c=c, carry=carry),
        grid=(gdim, t // rows),
        in_specs=[blk, blk, blk, blk, sblk, _const_spec((1, DV_REC))],
        out_specs=[blk, sblk],
        out_shape=[jax.ShapeDtypeStruct((gdim, t, W_REC), BF16), jax.ShapeDtypeStruct(s0.shape, F32)],
        scratch_shapes=[pltpu.VMEM((H_REC, DV_REC, DK_REC), F32)],
        compiler_params=pltpu.CompilerParams(dimension_semantics=("parallel", "arbitrary"),
                                             vmem_limit_bytes=V7X_VMEM_LIMIT_BYTES),
        name="hgrn",
    )(q, lf, v, g, s0, w_row)


def _rel_bucket(dist):
    dist = jnp.maximum(dist, 0)
    max_exact = NUM_BUCKETS // 2
    large = max_exact + (jnp.log(jnp.maximum(dist, 1).astype(F32) / max_exact)
                         / math.log(MAX_DISTANCE / max_exact) * (NUM_BUCKETS - max_exact)).astype(jnp.int32)
    large = jnp.minimum(large, NUM_BUCKETS - 1)
    return jnp.where(dist < max_exact, dist, large)


def _shifted_bias(rel_bias, dist):
    rb = rel_bias.astype(F32)
    rb = (rb - rb[NUM_BUCKETS - 1]) * LOG2E
    bucket = _rel_bucket(dist)[None]
    expand = (slice(None),) + (None,) * dist.ndim
    out = jnp.zeros((rb.shape[1],) + dist.shape, F32)
    for kb in range(NUM_BUCKETS - 1):
        out = jnp.where(bucket == kb, rb[kb][expand], out)
    return jnp.where((dist >= 0)[None], out, NEG)


def _split_halves(q):
    lane = lax.broadcasted_iota(jnp.int32, q.shape, 1)
    zero = jnp.zeros_like(q)
    return jnp.concatenate([jnp.where(lane < DK_ATT, q, zero), jnp.where(lane >= DK_ATT, q, zero)], axis=0)


def _attn_prompt_kernel(lam_ref, q_ref, k_ref, vt_ref, bd_ref, bs_ref, w_ref, o_ref, *, tq, lam_init):
    qi = pl.program_id(1)
    for h0 in range(0, H_ATT, ATT_HEAD_GROUP):
        _attn_prompt_heads(range(h0, h0 + ATT_HEAD_GROUP), qi, lam_ref, q_ref, k_ref, vt_ref, bd_ref, bs_ref,
                           w_ref, o_ref, tq, lam_init)


def _attn_prompt_heads(heads, qi, lam_ref, q_ref, k_ref, vt_ref, bd_ref, bs_ref, w_ref, o_ref, tq, lam_init):
    nt = 2 * tq // ATT_TILE
    cols = {h: slice(h * DV_ATT, (h + 1) * DV_ATT) for h in heads}
    q_tiles = {}
    for h in heads:
        qs = _split_halves(q_ref[0, :, cols[h]])
        q_tiles[h] = [qs[i * ATT_TILE:(i + 1) * ATT_TILE] for i in range(nt)]

    def scores(j):
        out = []
        for h in heads:
            k_blk = k_ref[0, pl.ds(pl.multiple_of(j * tq, tq), tq), cols[h]]
            out.append(tuple(lax.dot_general(k_blk, q_tiles[h][i], _NT, preferred_element_type=F32)
                             for i in range(nt)))
        return tuple(out)

    def update(j, s_all, bias_ref, extra, carry):
        out = []
        for g, h in enumerate(heads):
            vt_blk = vt_ref[j, cols[h], :]
            tiles = []
            for i in range(nt):
                m, l, acc = carry[g][i]
                s = s_all[g][i]
                if bias_ref is not None:
                    c0 = (i * ATT_TILE) % tq
                    s = s + bias_ref[h, :, c0:c0 + ATT_TILE]
                    if extra is not None:
                        s = s + extra
                m_new = jnp.maximum(m, jnp.max(s, axis=0, keepdims=True))
                alpha = jnp.exp2(m - m_new)
                p = jnp.exp2(s - m_new)
                l = l * alpha + jnp.sum(p, axis=0, keepdims=True)
                acc = acc * alpha + jnp.dot(vt_blk, p.astype(BF16), preferred_element_type=F32)
                tiles.append((m_new, l, acc))
            out.append(tuple(tiles))
        return tuple(out)

    init = tuple(tuple((jnp.full((1, ATT_TILE), NEG, F32), jnp.zeros((1, ATT_TILE), F32),
                        jnp.zeros((DV_ATT, ATT_TILE), F32)) for _ in range(nt)) for _ in heads)
    sub_j = jnp.maximum(qi - 1, 0)
    n_far = jnp.maximum(qi - 1, 0)
    s_diag = scores(qi)
    s_sub = scores(sub_j)
    carry = update(qi, s_diag, bd_ref, None, init)
    carry = update(sub_j, s_sub, bs_ref, jnp.where(qi == 0, NEG, 0.0), carry)

    def far_step(j, cr):
        return update(j, scores(j), None, None, cr)

    carry = lax.fori_loop(0, n_far, far_step, carry)
    for g, h in enumerate(heads):
        o_t = jnp.concatenate([acc / l for _, l, acc in carry[g]], axis=1)
        o_t = o_t[:, :tq] - lam_ref[0] * o_t[:, tq:]
        o_t = o_t * lax.rsqrt(jnp.mean(o_t * o_t, axis=0, keepdims=True) + EPS)
        o_ref[0, :, cols[h]] = (o_t.T * w_ref[...] * (1.0 - lam_init)).astype(BF16)


def _attn_prompt(lam, qa, k16, vt, bias_diag, bias_sub, w_row, tq, lam_init):
    b, s, _ = qa.shape
    nb = s // tq
    assert tq >= MAX_DISTANCE and s % tq == 0
    return pl.pallas_call(
        functools.partial(_attn_prompt_kernel, tq=tq, lam_init=lam_init),
        grid=(b, nb),
        in_specs=[pl.BlockSpec(memory_space=pltpu.SMEM),
                  pl.BlockSpec((1, tq, W_ATT), lambda bi, qi: (bi, qi, 0)),
                  pl.BlockSpec((1, s, W_ATT), lambda bi, qi: (bi, 0, 0)),
                  pl.BlockSpec((nb, W_ATT, tq), lambda bi, qi: (bi, 0, 0)),
                  _const_spec((H_ATT, tq, tq)), _const_spec((H_ATT, tq, tq)), _const_spec((1, DV_ATT))],
        out_specs=pl.BlockSpec((1, tq, W_ATT), lambda bi, qi: (bi, qi, 0)),
        out_shape=jax.ShapeDtypeStruct((b, s, W_ATT), BF16),
        compiler_params=pltpu.CompilerParams(dimension_semantics=("parallel", "arbitrary")),
        name="attn_prompt",
    )(lam, qa, k16, vt, bias_diag, bias_sub, w_row)


def _attn_sample_kernel(pt_ref, lam_ref, wp_ref, kn_ref, vn_ref, bnew_ref, btail_ref, w_ref, ck_hbm, cv_hbm,
                        o_ref, kbuf, vbuf, sem, m_scr, l_scr, acc_scr, *, t_new, lam_init):
    npg = PAGES_PER_STEP
    ahead = PAGE_SLOTS - 1
    j = pl.program_id(1)
    n_steps = pl.num_programs(1)
    g = pl.program_id(0) * n_steps + j
    last = pl.num_programs(0) * n_steps - 1
    slot = lax.rem(g, PAGE_SLOTS)
    rh = 2 * t_new
    half = H_ATT * rh
    ncol = 2 * half
    wpair = wp_ref[0]

    def page_copies(step, slot_):
        row, col0 = lax.div(step, n_steps), lax.rem(step, n_steps) * npg
        copies = []
        for i in range(npg):
            page = pt_ref[row, col0 + i]
            copies.append(pltpu.make_async_copy(ck_hbm.at[page], kbuf.at[slot_, i], sem.at[slot_]))
            copies.append(pltpu.make_async_copy(cv_hbm.at[page], vbuf.at[slot_, i], sem.at[slot_]))
        return copies

    def later_step(d):
        return jnp.minimum(g + d, last), lax.rem(g + d, PAGE_SLOTS)

    @pl.when(g == 0)
    def _():
        for d in range(ahead):
            for c in page_copies(*later_step(d)):
                c.start()

    for c in page_copies(g, slot):
        c.wait()
    k_refs = [kbuf.at[slot, i] for i in range(npg)]
    v_refs = [vbuf.at[slot, i] for i in range(npg)]

    def head_rows(ref, h):
        return ref[pl.ds(h, PAGE_SIZE, stride=H_ATT), :].astype(BF16)

    def rows_to_cols(row):
        return jnp.broadcast_to(row, (row.shape[1], row.shape[1])).T

    def weighted_values(p_t, get_v, parity):
        return [jnp.dot(p_t[parity * half + h * rh:parity * half + (h + 1) * rh], get_v(h),
                        preferred_element_type=F32) for h in range(H_ATT)]

    @pl.when(j == 0)
    def _():
        pad = PAGE_SIZE - t_new
        k_new = jnp.concatenate([kn_ref[0], jnp.zeros((pad, W_ATT), BF16)], axis=0)
        v_new = jnp.concatenate([vn_ref[0], jnp.zeros((pad, W_ATT), BF16)], axis=0)
        s = jnp.dot(k_new, wpair[:W_ATT, :], preferred_element_type=F32) + bnew_ref[...]
        m = jnp.max(s, axis=0, keepdims=True)
        even = lax.broadcasted_iota(jnp.int32, s.shape, 1) < half
        p = jnp.where(even, jnp.exp2(s - m), 0.0)
        m_scr[...] = jnp.where(even[:1], m, NEG)
        l_scr[...] = jnp.sum(p, axis=0, keepdims=True)
        p_t = p.T.astype(BF16)
        acc_scr[...] = jnp.concatenate(
            weighted_values(p_t, lambda h: v_new[:, h * DV_ATT:(h + 1) * DV_ATT], 0)
            + [jnp.zeros((half, DV_ATT), F32)], axis=0)

    for c in page_copies(*later_step(ahead)):
        c.start()
    tiles = []
    for pi in range(npg // 2):
        lhs = jnp.concatenate([head_rows(k_refs[2 * pi + par], h) for par in range(2) for h in range(H_ATT)],
                              axis=1)
        s = jnp.dot(lhs, wpair, preferred_element_type=F32)
        if pi == npg // 2 - 1:
            s = s + btail_ref[0]
        tiles.append(s)
    mx = tiles[0]
    for s in tiles[1:]:
        mx = jnp.maximum(mx, s)
    m_old = m_scr[...]
    m_new = jnp.maximum(m_old, jnp.max(mx, axis=0, keepdims=True))
    alpha = jnp.exp2(m_old - m_new)
    acc = acc_scr[...] * rows_to_cols(alpha)
    lsum = jnp.zeros_like(alpha)
    for pi in range(npg // 2):
        p = jnp.exp2(tiles[pi] - m_new)
        lsum = lsum + jnp.sum(p, axis=0, keepdims=True)
        p_t = p.T.astype(BF16)
        acc = acc + jnp.concatenate(
            [pv for par in range(2)
             for pv in weighted_values(p_t, functools.partial(head_rows, v_refs[2 * pi + par]), par)], axis=0)
    l = l_scr[...] * alpha + lsum
    m_scr[...] = m_new
    l_scr[...] = l
    acc_scr[...] = acc

    @pl.when(j == pl.num_programs(1) - 1)
    def _():
        m_c, l_c = rows_to_cols(m_new), rows_to_cols(l)
        m_f = jnp.maximum(m_c[:half], m_c[half:])
        w_e, w_o = jnp.exp2(m_c[:half] - m_f), jnp.exp2(m_c[half:] - m_f)
        o = (acc[:half] * w_e + acc[half:] * w_o) / (l_c[:half] * w_e + l_c[half:] * w_o)
        for h in range(H_ATT):
            oh = o[h * rh:h * rh + t_new] - lam_ref[0] * o[h * rh + t_new:(h + 1) * rh]
            oh = oh * lax.rsqrt(jnp.mean(oh * oh, axis=-1, keepdims=True) + EPS) * w_ref[...] * (1.0 - lam_init)
            o_ref[0, :, h * DV_ATT:(h + 1) * DV_ATT] = oh.astype(BF16)

    @pl.when(g == last)
    def _():
        for d in range(1, PAGE_SLOTS):
            for c in page_copies(*later_step(d)):
                c.wait()


def _attn_sample(page_table, lam, wpair, k_new, v_new, bias_new, bias_tail, w_row, cache_k, cache_v,
                 t_new, lam_init):
    nb, n_pages = page_table.shape
    npg = PAGES_PER_STEP
    assert n_pages % npg == 0 and npg % 2 == 0
    n_steps = n_pages // npg
    ncol = wpair.shape[2]
    prow = PAGE_SIZE * H_ATT
    per_b = lambda n, w: pl.BlockSpec((1, n, w), lambda b, j, pt: (b, 0, 0))
    const = lambda shape: pl.BlockSpec(shape, lambda b, j, pt: (0,) * len(shape))
    hbm = pl.BlockSpec(memory_space=pl.ANY)
    grid_spec = pltpu.PrefetchScalarGridSpec(
        num_scalar_prefetch=1,
        grid=(nb, n_steps),
        in_specs=[pl.BlockSpec(memory_space=pltpu.SMEM), per_b(2 * W_ATT, ncol), per_b(t_new, W_ATT),
                  per_b(t_new, W_ATT), const((PAGE_SIZE, ncol)),
                  pl.BlockSpec((1, PAGE_SIZE, ncol), lambda b, j, pt: (jnp.where(j == n_steps - 1, 1, 0), 0, 0)),
                  const((1, DV_ATT)), hbm, hbm],
        out_specs=pl.BlockSpec((1, t_new, W_ATT), lambda b, j, pt: (b, 0, 0)),
        scratch_shapes=[pltpu.VMEM((PAGE_SLOTS, npg, prow, DV_ATT), F32),
                        pltpu.VMEM((PAGE_SLOTS, npg, prow, DV_ATT), F32),
                        pltpu.SemaphoreType.DMA((PAGE_SLOTS,)),
                        pltpu.VMEM((1, ncol), F32), pltpu.VMEM((1, ncol), F32), pltpu.VMEM((ncol, DV_ATT), F32)],
    )
    return pl.pallas_call(
        functools.partial(_attn_sample_kernel, t_new=t_new, lam_init=lam_init),
        grid_spec=grid_spec,
        out_shape=jax.ShapeDtypeStruct((nb, t_new, W_ATT), BF16),
        compiler_params=pltpu.CompilerParams(dimension_semantics=("arbitrary", "arbitrary"),
                                             vmem_limit_bytes=V7X_VMEM_LIMIT_BYTES),
        name="attn_sample",
    )(page_table, lam, wpair, k_new, v_new, bias_new, bias_tail, w_row, cache_k, cache_v)


def _out_mlp_kernel(x_ref, yr_ref, ya_ref, ga_ref, gb_ref, wa_ref, wb_ref, wo_ref, n2_ref, wu_ref, wd_ref,
                    y_ref, *, ff_chunk):
    a = jnp.dot(yr_ref[...], wa_ref[...], preferred_element_type=F32)
    b = jnp.dot(ya_ref[...], wb_ref[...], preferred_element_type=F32)
    mixed = ga_ref[...].astype(F32) * a + gb_ref[...].astype(F32) * b
    h = x_ref[...] + jnp.dot(mixed.astype(BF16), wo_ref[...], preferred_element_type=F32)
    ms = jnp.mean(h * h, axis=-1, keepdims=True)
    hn = (h * lax.rsqrt(ms + EPS) * n2_ref[...]).astype(BF16)
    y = h
    for c0 in range(0, D_FF, ff_chunk):
        u = jnp.maximum(jnp.dot(hn, wu_ref[:, c0:c0 + ff_chunk], preferred_element_type=F32), 0.0)
        y = y + jnp.dot((u * u).astype(BF16), wd_ref[c0:c0 + ff_chunk, :], preferred_element_type=F32)
    y_ref[...] = y


def _out_mlp(x2d, y_rec, y_att, ga, gb, wa16, wb16, wo16, n2, wu16, wd16, tm):
    m = x2d.shape[0]
    row = lambda w: pl.BlockSpec((tm, w), lambda i: (i, 0))
    return pl.pallas_call(
        functools.partial(_out_mlp_kernel, ff_chunk=D_MODEL),
        grid=(m // tm,),
        in_specs=[row(D_MODEL), row(W_REC), row(W_ATT), row(D_MODEL), row(D_MODEL),
                  _const_spec((W_REC, D_MODEL)), _const_spec((W_ATT, D_MODEL)), _const_spec((D_MODEL, D_MODEL)),
                  _const_spec((1, D_MODEL)), _const_spec((D_MODEL, D_FF)), _const_spec((D_FF, D_MODEL))],
        out_specs=row(D_MODEL),
        out_shape=jax.ShapeDtypeStruct((m, D_MODEL), F32),
        compiler_params=pltpu.CompilerParams(dimension_semantics=("parallel",),
                                             vmem_limit_bytes=V7X_VMEM_LIMIT_BYTES),
        name="out_mlp",
    )(x2d, y_rec, y_att, ga, gb, wa16, wb16, wo16, n2, wu16, wd16)


def kernel(x_prompt, x_sample, cache_k, cache_v, state_hgrn, page_table, norm1_w, w_in, lower_bounds, hgrn_norm_w, q_norm_w, k_norm_w, lam_q1, lam_k1, lam_q2, lam_k2, subln_w, rel_bias, w_branch_a, w_branch_b, w_out, norm2_w, w_up, w_down):
    depth = w_in.shape[0]
    assert depth == 1
    bp, sp, _ = x_prompt.shape
    bs, ts, _ = x_sample.shape
    n_pages = page_table.shape[1]
    past = n_pages * PAGE_SIZE
    assert PAGE_SIZE >= MAX_DISTANCE - 1 and ts <= HGRN_CHUNK
    l = 0
    lam_init = 0.8 - 0.6 * math.exp(-0.3 * l)
    scale = DK_ATT ** -0.5

    lb = jnp.cumsum(jax.nn.softmax(lower_bounds.astype(F32), axis=0), axis=0)[l][None, :]
    lam = (jnp.exp(jnp.dot(lam_q1[l].astype(F32), lam_k1[l].astype(F32)))
           - jnp.exp(jnp.dot(lam_q2[l].astype(F32), lam_k2[l].astype(F32))) + lam_init).reshape(1)
    n1 = norm1_w[l].astype(F32)[None, :]
    n2 = norm2_w[l].astype(F32)[None, :]
    qn_row = jnp.tile(q_norm_w[l].astype(F32).reshape(-1), H_ATT)[None, :] * (scale * LOG2E)
    kn_row = jnp.tile(k_norm_w[l].astype(F32).reshape(-1), H_ATT)[None, :]
    hw_row = hgrn_norm_w[l].astype(F32)[None, :]
    sw_row = subln_w[l].astype(F32)[None, :]
    seg = jnp.arange(W_ATT) // DK_ATT
    pseg = jnp.where(seg[:, None] == seg[None, :], 1.0 / DK_ATT, 0.0).astype(BF16)
    w_in16, wa16, wb16 = w_in[l].astype(BF16), w_branch_a[l].astype(BF16), w_branch_b[l].astype(BF16)
    wo16, wu16, wd16 = w_out[l].astype(BF16), w_up[l].astype(BF16), w_down[l].astype(BF16)

    def layer(x, attend, hgrn_fn, tm, kb):
        b, t, _ = x.shape
        x2d = x.reshape(b * t, D_MODEL)
        q, lf, vr, g, qa, k32, k16, v32, vt, ga, gb = _in_proj(x2d, n1, w_in16, lb, qn_row, kn_row, pseg, tm, kb)
        y_rec, s_new = hgrn_fn(q, lf, vr, g)
        y_att = attend(qa, k16, vt)
        y = _out_mlp(x2d, y_rec.reshape(b * t, W_REC), y_att.reshape(b * t, W_ATT), ga, gb,
                     wa16, wb16, wo16, n2, wu16, wd16, tm)
        return (y.reshape(b, t, D_MODEL), k32.reshape(1, b, t, H_ATT, DV_ATT),
                v32.reshape(1, b, t, H_ATT, DV_ATT), s_new[None])

    tq = ATT_BLOCK
    ri = jnp.arange(tq)
    bias_diag = _shifted_bias(rel_bias, ri[None, :] - ri[:, None])
    bias_sub = _shifted_bias(rel_bias, ri[None, :] + tq - ri[:, None])
    r3p = lambda a: a.reshape(bp, sp, a.shape[-1])
    attend_p = lambda qa, k16, vt: _attn_prompt(lam, r3p(qa), r3p(k16), vt, bias_diag, bias_sub, sw_row, tq, lam_init)
    s0p = jnp.zeros((bp, H_REC, DK_REC, DV_REC), F32)
    hgrn_p = lambda q, lf, vr, g: _hgrn(r3p(q), r3p(lf), r3p(vr), r3p(g), s0p, hw_row, 256, HGRN_CHUNK, True)
    y_p, k_p, v_p, s_p = layer(x_prompt, attend_p, hgrn_p, 512, tq)

    half = H_ATT * 2 * ts
    col_h = jnp.arange(half) // (2 * ts)
    col_t = jnp.arange(half) % ts
    assert 2 * half == V7X_LANES and n_pages % 2 == 0

    def own_head(bias):
        out = jnp.zeros(bias.shape[1:], F32)
        for h in range(H_ATT):
            out = jnp.where((col_h == h)[None, :], bias[h], out)
        return out

    keys = jnp.arange(PAGE_SIZE)
    tail_dist = (past + col_t)[None, :] - ((n_pages - 1) * PAGE_SIZE + keys)[:, None]
    zeros_half = jnp.zeros((PAGE_SIZE, half), F32)
    bias_tail = jnp.stack([jnp.zeros((PAGE_SIZE, 2 * half), F32),
                           jnp.concatenate([zeros_half, own_head(_shifted_bias(rel_bias, tail_dist))], axis=1)])
    new_dist = jnp.where((keys < ts)[:, None], col_t[None, :] - keys[:, None], -1)
    bias_new = jnp.concatenate([own_head(_shifted_bias(rel_bias, new_dist)), zeros_half], axis=1)
    ck = cache_k[l].reshape(cache_k.shape[1], PAGE_SIZE * H_ATT, DV_ATT)
    cv = cache_v[l].reshape(cache_v.shape[1], PAGE_SIZE * H_ATT, DV_ATT)

    def attend_s(qa, k16, vt):
        q4 = qa.reshape(bs, ts, H_ATT, DV_ATT).transpose(0, 2, 3, 1)
        d = jnp.arange(DV_ATT)[:, None]
        qcols = jnp.concatenate([jnp.where(d < DK_ATT, q4, 0), jnp.where(d >= DK_ATT, q4, 0)], axis=3)
        eye = jnp.eye(H_ATT, dtype=qcols.dtype)
        wstack = (qcols[:, :, :, None, :] * eye[None, :, None, :, None]).reshape(bs, W_ATT, half)
        zero = jnp.zeros_like(wstack)
        wpair = jnp.concatenate([jnp.concatenate([wstack, zero], axis=2),
                                 jnp.concatenate([zero, wstack], axis=2)], axis=1).astype(BF16)
        return _attn_sample(page_table, lam, wpair, k16.reshape(bs, ts, W_ATT), vt.reshape(bs, ts, W_ATT),
                            bias_new, bias_tail, sw_row, ck, cv, ts, lam_init)

    r3s = lambda a: a.reshape(1, bs * ts, a.shape[-1])
    hgrn_rows = 16 * ts
    hgrn_s = lambda q, lf, vr, g: _hgrn(r3s(q), r3s(lf), r3s(vr), r3s(g), state_hgrn[l].astype(F32), hw_row,
                                        hgrn_rows, ts, False)
    y_s, k_s, v_s, s_s = layer(x_sample, attend_s, hgrn_s, 512, None)
    return (y_p, y_s, k_p, v_p, s_p, k_s, v_s, s_s)
```

```python
import functools
import math

import jax
import jax.numpy as jnp
from jax import lax
from jax.experimental import pallas as pl
from jax.experimental.pallas import tpu as pltpu

F32 = jnp.float32
BF16 = jnp.bfloat16

D_MODEL = 1024
H_REC, DK_REC, DV_REC = 4, 128, 128
W_REC = H_REC * DK_REC
H_ATT, DK_ATT = 4, 64
DV_ATT = 2 * DK_ATT
W_ATT = H_ATT * DV_ATT
D_FF = 4 * D_MODEL
NUM_BUCKETS = 32
MAX_DISTANCE = 128
PAGE_SIZE = 128
HGRN_CHUNK = 32
EPS = 1e-6
NEG = -1e30
LOG2E = math.log2(math.e)

OFF_HQ, OFF_HF, OFF_HI, OFF_HG = 0, W_REC, 2 * W_REC, 3 * W_REC
OFF_AQ = 4 * W_REC
OFF_AK = OFF_AQ + W_ATT
OFF_AV = OFF_AK + W_ATT
OFF_GA = OFF_AV + W_ATT
OFF_GB = OFF_GA + D_MODEL
IN_WIDTH = OFF_GB + D_MODEL

V7X_VMEM_LIMIT_BYTES = 56 * 1024 * 1024
V7X_LANES = 128
PAGES_PER_STEP = 16
PAGE_SLOTS = 3
ATT_BLOCK = 256
ATT_TILE = 128
ATT_HEAD_GROUP = 4

_NT = (((1,), (1,)), ((), ()))


def _const_spec(shape):
    n = len(shape)
    return pl.BlockSpec(shape, lambda *_: (0,) * n, pipeline_mode=pl.Buffered(1))


def _sigmoid(x):
    return 1.0 / (1.0 + jnp.exp(-x))


def _in_proj_kernel(x_ref, n1_ref, w_ref, lb_ref, qn_ref, kn_ref, pseg_ref,
                    q_ref, lf_ref, vr_ref, g_ref, qa_ref, k32_ref, k16_ref, v32_ref, vt_ref,
                    ga_ref, gb_ref):
    tm = x_ref.shape[0]
    x = x_ref[...]
    ms = jnp.mean(x * x, axis=-1, keepdims=True)
    xb = (x * lax.rsqrt(ms + EPS) * n1_ref[...]).astype(BF16)

    def proj(off, size):
        return jnp.dot(xb, w_ref[:, off:off + size], preferred_element_type=F32)

    def seg_rms(z):
        m = jnp.dot((z * z).astype(BF16), pseg_ref[...], preferred_element_type=F32)
        return z * lax.rsqrt(m + EPS)

    def store_head_rows(ref, z):
        for h in range(H_ATT):
            ref[pl.ds(h, tm, stride=H_ATT), :] = z[:, h * DV_ATT:(h + 1) * DV_ATT]

    hq = proj(OFF_HQ, W_REC)
    q_ref[...] = (hq * _sigmoid(hq)).astype(BF16)
    hf = proj(OFF_HF, W_REC)
    lb = lb_ref[...]
    lf_ref[...] = jnp.log(lb + (1.0 - lb) * _sigmoid(hf))
    vr_ref[...] = proj(OFF_HI, W_REC).astype(BF16)
    hg = proj(OFF_HG, W_REC)
    g_ref[...] = (hg * _sigmoid(hg)).astype(BF16)

    qa_ref[...] = (seg_rms(proj(OFF_AQ, W_ATT)) * qn_ref[...]).astype(BF16)
    ka = seg_rms(proj(OFF_AK, W_ATT)) * kn_ref[...]
    store_head_rows(k32_ref, ka)
    k16_ref[...] = ka.astype(BF16)
    va = proj(OFF_AV, W_ATT)
    store_head_rows(v32_ref, va)
    if len(vt_ref.shape) == 2:
        vt_ref[...] = va.astype(BF16)
    else:
        kb = vt_ref.shape[2]
        for i in range(vt_ref.shape[0]):
            vt_ref[i] = va[i * kb:(i + 1) * kb, :].T.astype(BF16)
    ga_ref[...] = _sigmoid(proj(OFF_GA, D_MODEL)).astype(BF16)
    gb_ref[...] = _sigmoid(proj(OFF_GB, D_MODEL)).astype(BF16)


def _in_proj(x2d, n1, w_in16, lb, qn_row, kn_row, pseg, tm, kb):
    m = x2d.shape[0]
    row = lambda w: pl.BlockSpec((tm, w), lambda i: (i, 0))
    head_rows = pl.BlockSpec((tm * H_ATT, DV_ATT), lambda i: (i, 0))
    if kb is None:
        vt_spec, vt_shape = row(W_ATT), (m, W_ATT)
    else:
        vt_spec, vt_shape = pl.BlockSpec((tm // kb, W_ATT, kb), lambda i: (i, 0, 0)), (m // kb, W_ATT, kb)
    sds = jax.ShapeDtypeStruct
    return pl.pallas_call(
        _in_proj_kernel,
        grid=(m // tm,),
        in_specs=[row(D_MODEL), _const_spec((1, D_MODEL)), _const_spec((D_MODEL, IN_WIDTH)),
                  _const_spec((1, W_REC)), _const_spec((1, W_ATT)), _const_spec((1, W_ATT)),
                  _const_spec((W_ATT, W_ATT))],
        out_specs=[row(W_REC), row(W_REC), row(W_REC), row(W_REC), row(W_ATT), head_rows, row(W_ATT),
                   head_rows, vt_spec, row(D_MODEL), row(D_MODEL)],
        out_shape=[sds((m, W_REC), BF16), sds((m, W_REC), F32), sds((m, W_REC), BF16), sds((m, W_REC), BF16),
                   sds((m, W_ATT), BF16), sds((m * H_ATT, DV_ATT), F32), sds((m, W_ATT), BF16),
                   sds((m * H_ATT, DV_ATT), F32), sds(vt_shape, BF16),
                   sds((m, D_MODEL), BF16), sds((m, D_MODEL), BF16)],
        compiler_params=pltpu.CompilerParams(dimension_semantics=("parallel",),
                                             vmem_limit_bytes=V7X_VMEM_LIMIT_BYTES),
        name="in_proj",
    )(x2d, n1, w_in16, lb, qn_row, kn_row, pseg)


def _hgrn_kernel(q_ref, lf_ref, v_ref, g_ref, s0_ref, w_ref, y_ref, s_ref, st_scr, *, c, carry):
    r = q_ref.shape[1]
    nc = r // c
    t = pl.program_id(1)
    mm = BF16 if c >= 16 else F32
    shift = c.bit_length() - 1
    r_i = lax.broadcasted_iota(jnp.int32, (r, r), 0)
    c_i = lax.broadcasted_iota(jnp.int32, (r, r), 1)
    tril = ((r_i >> shift) == (c_i >> shift)) & (r_i >= c_i)
    tril16 = jnp.where(tril, 1.0, 0.0).astype(BF16)
    lane_chunk = lax.broadcasted_iota(jnp.int32, (DK_REC, r), 1) >> shift

    def rows_to_cols(row):
        return jnp.broadcast_to(row, (row.shape[1], row.shape[1])).T

    if carry:
        @pl.when(t == 0)
        def _():
            for h in range(H_REC):
                st_scr[h] = s0_ref[0, h]

    heads = range(H_REC)
    cols = [slice(h * DK_REC, (h + 1) * DK_REC) for h in heads]
    lf = [lf_ref[0, :, cols[h]] for h in heads]
    v16 = [v_ref[0, :, cols[h]] for h in heads]
    b = []
    for h in heads:
        hi = lf[h].astype(BF16)
        lo = (lf[h] - hi.astype(F32)).astype(BF16)
        b.append(jnp.dot(tril16, hi, preferred_element_type=F32) + jnp.dot(tril16, lo, preferred_element_type=F32))
    qe, qe16, ke16, kd, dec = [], [], [], [], []
    for h in heads:
        b3 = b[h].reshape(nc, c, DK_REC)
        bl = b3[:, c - 1:c, :]
        k = 1.0 - jnp.exp(lf[h])
        qe.append(q_ref[0, :, cols[h]].astype(F32) * jnp.exp(b[h]))
        qe16.append(qe[h].astype(BF16))
        ke16.append((k * jnp.exp(-b[h])).astype(BF16))
        kd.append((k.reshape(nc, c, DK_REC) * jnp.exp(bl - b3)).reshape(r, DK_REC))
        dec.append(jnp.exp(bl))
    a = [lax.dot_general(qe16[h], ke16[h], _NT, preferred_element_type=F32) for h in heads]
    u = []
    for h in heads:
        kdt = kd[h].T
        lhs = jnp.concatenate([jnp.where(lane_chunk == n, kdt, 0.0) for n in range(nc)], axis=0).astype(BF16)
        u.append(jnp.dot(lhs, v16[h], preferred_element_type=F32))
    o = [jnp.dot(jnp.where(tril, a[h], 0.0).astype(BF16), v16[h], preferred_element_type=F32) for h in heads]
    qe_mm = qe16 if mm == BF16 else qe
    parts = [[] for _ in heads]
    st = [st_scr[h] for h in heads] if carry else None
    for n in range(nc):
        rows = slice(n * c, (n + 1) * c)
        for h in heads:
            s_in = st[h] if carry else s0_ref[n, h]
            parts[h].append(jnp.dot(qe_mm[h][rows], s_in.astype(mm), preferred_element_type=F32))
            s_out = s_in * rows_to_cols(dec[h][n]) + u[h][n * DK_REC:(n + 1) * DK_REC]
            if carry:
                st[h] = s_out
            else:
                s_ref[n, h] = s_out
    for h in heads:
        if carry:
            st_scr[h] = st[h]
        oh = o[h] + jnp.concatenate(parts[h], axis=0)
        on = oh * lax.rsqrt(jnp.mean(oh * oh, axis=-1, keepdims=True) + EPS) * w_ref[...]
        y_ref[0, :, cols[h]] = (on * g_ref[0, :, cols[h]].astype(F32)).astype(BF16)

    if carry:
        @pl.when(t == pl.num_programs(1) - 1)
        def _():
            for h in range(H_REC):
                s_ref[0, h] = st_scr[h]


def _hgrn(q, lf, v, g, s0, w_row, rows, c, carry):
    gdim, t, _ = q.shape
    nseq = 1 if carry else rows // c
    blk = pl.BlockSpec((1, rows, W_REC), lambda gi, ti: (gi, ti, 0))
    if carry:
        sblk = pl.BlockSpec((1, H_REC, DK_REC, DV_REC), lambda gi, ti: (gi, 0, 0, 0))
    else:
        sblk = pl.BlockSpec((nseq, H_REC, DK_REC, DV_REC), lambda gi, ti: (ti, 0, 0, 0))
    return pl.pallas_call(
        functools.partial(_hgrn_kernel, c=c, carry=carry),
        grid=(gdim, t // rows),
        in_specs=[blk, blk, blk, blk, sblk, _const_spec((1, DV_REC))],
        out_specs=[blk, sblk],
        out_shape=[jax.ShapeDtypeStruct((gdim, t, W_REC), BF16), jax.ShapeDtypeStruct(s0.shape, F32)],
        scratch_shapes=[pltpu.VMEM((H_REC, DK_REC, DV_REC), F32)],
        compiler_params=pltpu.CompilerParams(dimension_semantics=("parallel", "arbitrary"),
                                             vmem_limit_bytes=V7X_VMEM_LIMIT_BYTES),
        name="hgrn",
    )(q, lf, v, g, s0, w_row)


def _rel_bucket(dist):
    dist = jnp.maximum(dist, 0)
    max_exact = NUM_BUCKETS // 2
    large = max_exact + (jnp.log(jnp.maximum(dist, 1).astype(F32) / max_exact)
                         / math.log(MAX_DISTANCE / max_exact) * (NUM_BUCKETS - max_exact)).astype(jnp.int32)
    large = jnp.minimum(large, NUM_BUCKETS - 1)
    return jnp.where(dist < max_exact, dist, large)


def _shifted_bias(rel_bias, dist):
    rb = rel_bias.astype(F32)
    rb = (rb - rb[NUM_BUCKETS - 1]) * LOG2E
    bucket = _rel_bucket(dist)[None]
    expand = (slice(None),) + (None,) * dist.ndim
    out = jnp.zeros((rb.shape[1],) + dist.shape, F32)
    for kb in range(NUM_BUCKETS - 1):
        out = jnp.where(bucket == kb, rb[kb][expand], out)
    return jnp.where((dist >= 0)[None], out, NEG)


def _split_halves(q):
    lane = lax.broadcasted_iota(jnp.int32, q.shape, 1)
    zero = jnp.zeros_like(q)
    return jnp.concatenate([jnp.where(lane < DK_ATT, q, zero), jnp.where(lane >= DK_ATT, q, zero)], axis=0)


def _attn_prompt_kernel(lam_ref, q_ref, k_ref, vt_ref, bd_ref, bs_ref, w_ref, o_ref, *, tq, lam_init):
    qi = pl.program_id(1)
    for h0 in range(0, H_ATT, ATT_HEAD_GROUP):
        _attn_prompt_heads(range(h0, h0 + ATT_HEAD_GROUP), qi, lam_ref, q_ref, k_ref, vt_ref, bd_ref, bs_ref,
                           w_ref, o_ref, tq, lam_init)


def _attn_prompt_heads(heads, qi, lam_ref, q_ref, k_ref, vt_ref, bd_ref, bs_ref, w_ref, o_ref, tq, lam_init):
    nt = 2 * tq // ATT_TILE
    cols = {h: slice(h * DV_ATT, (h + 1) * DV_ATT) for h in heads}
    q_tiles = {}
    for h in heads:
        qs = _split_halves(q_ref[0, :, cols[h]])
        q_tiles[h] = [qs[i * ATT_TILE:(i + 1) * ATT_TILE] for i in range(nt)]

    def scores(j):
        out = []
        for h in heads:
            k_blk = k_ref[0, pl.ds(pl.multiple_of(j * tq, tq), tq), cols[h]]
            out.append(tuple(lax.dot_general(k_blk, q_tiles[h][i], _NT, preferred_element_type=F32)
                             for i in range(nt)))
        return tuple(out)

    def update(j, s_all, bias_ref, extra, carry):
        out = []
        for g, h in enumerate(heads):
            vt_blk = vt_ref[j, cols[h], :]
            tiles = []
            for i in range(nt):
                m, l, acc = carry[g][i]
                s = s_all[g][i]
                if bias_ref is not None:
                    c0 = (i * ATT_TILE) % tq
                    s = s + bias_ref[h, :, c0:c0 + ATT_TILE]
                    if extra is not None:
                        s = s + extra
                m_new = jnp.maximum(m, jnp.max(s, axis=0, keepdims=True))
                alpha = jnp.exp2(m - m_new)
                p = jnp.exp2(s - m_new)
                l = l * alpha + jnp.sum(p, axis=0, keepdims=True)
                acc = acc * alpha + jnp.dot(vt_blk, p.astype(BF16), preferred_element_type=F32)
                tiles.append((m_new, l, acc))
            out.append(tuple(tiles))
        return tuple(out)

    init = tuple(tuple((jnp.full((1, ATT_TILE), NEG, F32), jnp.zeros((1, ATT_TILE), F32),
                        jnp.zeros((DV_ATT, ATT_TILE), F32)) for _ in range(nt)) for _ in heads)
    sub_j = jnp.maximum(qi - 1, 0)
    n_far = jnp.maximum(qi - 1, 0)
    s_diag = scores(qi)
    s_sub = scores(sub_j)
    carry = update(qi, s_diag, bd_ref, None, init)
    carry = update(sub_j, s_sub, bs_ref, jnp.where(qi == 0, NEG, 0.0), carry)

    def far_step(j, cr):
        return update(j, scores(j), None, None, cr)

    carry = lax.fori_loop(0, n_far, far_step, carry)
    for g, h in enumerate(heads):
        o_t = jnp.concatenate([acc / l for _, l, acc in carry[g]], axis=1)
        o_t = o_t[:, :tq] - lam_ref[0] * o_t[:, tq:]
        o_t = o_t * lax.rsqrt(jnp.mean(o_t * o_t, axis=0, keepdims=True) + EPS)
        o_ref[0, :, cols[h]] = (o_t.T * w_ref[...] * (1.0 - lam_init)).astype(BF16)


def _attn_prompt(lam, qa, k16, vt, bias_diag, bias_sub, w_row, tq, lam_init):
    b, s, _ = qa.shape
    nb = s // tq
    assert tq >= MAX_DISTANCE and s % tq == 0
    return pl.pallas_call(
        functools.partial(_attn_prompt_kernel, tq=tq, lam_init=lam_init),
        grid=(b, nb),
        in_specs=[pl.BlockSpec(memory_space=pltpu.SMEM),
                  pl.BlockSpec((1, tq, W_ATT), lambda bi, qi: (bi, qi, 0)),
                  pl.BlockSpec((1, s, W_ATT), lambda bi, qi: (bi, 0, 0)),
                  pl.BlockSpec((nb, W_ATT, tq), lambda bi, qi: (bi, 0, 0)),
                  _const_spec((H_ATT, tq, tq)), _const_spec((H_ATT, tq, tq)), _const_spec((1, DV_ATT))],
        out_specs=pl.BlockSpec((1, tq, W_ATT), lambda bi, qi: (bi, qi, 0)),
        out_shape=jax.ShapeDtypeStruct((b, s, W_ATT), BF16),
        compiler_params=pltpu.CompilerParams(dimension_semantics=("parallel", "arbitrary")),
        name="attn_prompt",
    )(lam, qa, k16, vt, bias_diag, bias_sub, w_row)


def _attn_sample_kernel(pt_ref, lam_ref, wp_ref, kn_ref, vn_ref, bnew_ref, btail_ref, w_ref, ck_hbm, cv_hbm,
                        o_ref, kbuf, vbuf, sem, m_scr, l_scr, acc_scr, *, t_new, lam_init):
    npg = PAGES_PER_STEP
    ahead = PAGE_SLOTS - 1
    j = pl.program_id(1)
    n_steps = pl.num_programs(1)
    g = pl.program_id(0) * n_steps + j
    last = pl.num_programs(0) * n_steps - 1
    slot = lax.rem(g, PAGE_SLOTS)
    rh = 2 * t_new
    half = H_ATT * rh
    ncol = 2 * half
    wpair = wp_ref[0]

    def page_copies(step, slot_):
        row, col0 = lax.div(step, n_steps), lax.rem(step, n_steps) * npg
        copies = []
        for i in range(npg):
            page = pt_ref[row, col0 + i]
            copies.append(pltpu.make_async_copy(ck_hbm.at[page], kbuf.at[slot_, i], sem.at[slot_]))
            copies.append(pltpu.make_async_copy(cv_hbm.at[page], vbuf.at[slot_, i], sem.at[slot_]))
        return copies

    def later_step(d):
        return jnp.minimum(g + d, last), lax.rem(g + d, PAGE_SLOTS)

    @pl.when(g == 0)
    def _():
        for d in range(ahead):
            for c in page_copies(*later_step(d)):
                c.start()

    for c in page_copies(g, slot):
        c.wait()
    k_refs = [kbuf.at[slot, i] for i in range(npg)]
    v_refs = [vbuf.at[slot, i] for i in range(npg)]

    def head_rows(ref, h):
        return ref[pl.ds(h, PAGE_SIZE, stride=H_ATT), :].astype(BF16)

    def rows_to_cols(row):
        return jnp.broadcast_to(row, (row.shape[1], row.shape[1])).T

    def weighted_values(p_t, get_v, parity):
        return [jnp.dot(p_t[parity * half + h * rh:parity * half + (h + 1) * rh], get_v(h),
                        preferred_element_type=F32) for h in range(H_ATT)]

    @pl.when(j == 0)
    def _():
        pad = PAGE_SIZE - t_new
        k_new = jnp.concatenate([kn_ref[0], jnp.zeros((pad, W_ATT), BF16)], axis=0)
        v_new = jnp.concatenate([vn_ref[0], jnp.zeros((pad, W_ATT), BF16)], axis=0)
        s = jnp.dot(k_new, wpair[:W_ATT, :], preferred_element_type=F32) + bnew_ref[...]
        m = jnp.max(s, axis=0, keepdims=True)
        even = lax.broadcasted_iota(jnp.int32, s.shape, 1) < half
        p = jnp.where(even, jnp.exp2(s - m), 0.0)
        m_scr[...] = jnp.where(even[:1], m, NEG)
        l_scr[...] = jnp.sum(p, axis=0, keepdims=True)
        p_t = p.T.astype(BF16)
        acc_scr[...] = jnp.concatenate(
            weighted_values(p_t, lambda h: v_new[:, h * DV_ATT:(h + 1) * DV_ATT], 0)
            + [jnp.zeros((half, DV_ATT), F32)], axis=0)

    for c in page_copies(*later_step(ahead)):
        c.start()
    tiles = []
    for pi in range(npg // 2):
        lhs = jnp.concatenate([head_rows(k_refs[2 * pi + par], h) for par in range(2) for h in range(H_ATT)],
                              axis=1)
        s = jnp.dot(lhs, wpair, preferred_element_type=F32)
        if pi == npg // 2 - 1:
            s = s + btail_ref[0]
        tiles.append(s)
    mx = tiles[0]
    for s in tiles[1:]:
        mx = jnp.maximum(mx, s)
    m_old = m_scr[...]
    m_new = jnp.maximum(m_old, jnp.max(mx, axis=0, keepdims=True))
    alpha = jnp.exp2(m_old - m_new)
    acc = acc_scr[...] * rows_to_cols(alpha)
    lsum = jnp.zeros_like(alpha)
    for pi in range(npg // 2):
        p = jnp.exp2(tiles[pi] - m_new)
        lsum = lsum + jnp.sum(p, axis=0, keepdims=True)
        p_t = p.T.astype(BF16)
        acc = acc + jnp.concatenate(
            [pv for par in range(2)
             for pv in weighted_values(p_t, functools.partial(head_rows, v_refs[2 * pi + par]), par)], axis=0)
    l = l_scr[...] * alpha + lsum
    m_scr[...] = m_new
    l_scr[...] = l
    acc_scr[...] = acc

    @pl.when(j == pl.num_programs(1) - 1)
    def _():
        m_c, l_c = rows_to_cols(m_new), rows_to_cols(l)
        m_f = jnp.maximum(m_c[:half], m_c[half:])
        w_e, w_o = jnp.exp2(m_c[:half] - m_f), jnp.exp2(m_c[half:] - m_f)
        o = (acc[:half] * w_e + acc[half:] * w_o) / (l_c[:half] * w_e + l_c[half:] * w_o)
        for h in range(H_ATT):
            oh = o[h * rh:h * rh + t_new] - lam_ref[0] * o[h * rh + t_new:(h + 1) * rh]
            oh = oh * lax.rsqrt(jnp.mean(oh * oh, axis=-1, keepdims=True) + EPS) * w_ref[...] * (1.0 - lam_init)
            o_ref[0, :, h * DV_ATT:(h + 1) * DV_ATT] = oh.astype(BF16)

    @pl.when(g == last)
    def _():
        for d in range(1, PAGE_SLOTS):
            for c in page_copies(*later_step(d)):
                c.wait()


def _attn_sample(page_table, lam, wpair, k_new, v_new, bias_new, bias_tail, w_row, cache_k, cache_v,
                 t_new, lam_init):
    nb, n_pages = page_table.shape
    npg = PAGES_PER_STEP
    assert n_pages % npg == 0 and npg % 2 == 0
    n_steps = n_pages // npg
    ncol = wpair.shape[2]
    prow = PAGE_SIZE * H_ATT
    per_b = lambda n, w: pl.BlockSpec((1, n, w), lambda b, j, pt: (b, 0, 0))
    const = lambda shape: pl.BlockSpec(shape, lambda b, j, pt: (0,) * len(shape))
    hbm = pl.BlockSpec(memory_space=pl.ANY)
    grid_spec = pltpu.PrefetchScalarGridSpec(
        num_scalar_prefetch=1,
        grid=(nb, n_steps),
        in_specs=[pl.BlockSpec(memory_space=pltpu.SMEM), per_b(2 * W_ATT, ncol), per_b(t_new, W_ATT),
                  per_b(t_new, W_ATT), const((PAGE_SIZE, ncol)),
                  pl.BlockSpec((1, PAGE_SIZE, ncol), lambda b, j, pt: (jnp.where(j == n_steps - 1, 1, 0), 0, 0)),
                  const((1, DV_ATT)), hbm, hbm],
        out_specs=pl.BlockSpec((1, t_new, W_ATT), lambda b, j, pt: (b, 0, 0)),
        scratch_shapes=[pltpu.VMEM((PAGE_SLOTS, npg, prow, DV_ATT), F32),
                        pltpu.VMEM((PAGE_SLOTS, npg, prow, DV_ATT), F32),
                        pltpu.SemaphoreType.DMA((PAGE_SLOTS,)),
                        pltpu.VMEM((1, ncol), F32), pltpu.VMEM((1, ncol), F32), pltpu.VMEM((ncol, DV_ATT), F32)],
    )
    return pl.pallas_call(
        functools.partial(_attn_sample_kernel, t_new=t_new, lam_init=lam_init),
        grid_spec=grid_spec,
        out_shape=jax.ShapeDtypeStruct((nb, t_new, W_ATT), BF16),
        compiler_params=pltpu.CompilerParams(dimension_semantics=("arbitrary", "arbitrary"),
                                             vmem_limit_bytes=V7X_VMEM_LIMIT_BYTES),
        name="attn_sample",
    )(page_table, lam, wpair, k_new, v_new, bias_new, bias_tail, w_row, cache_k, cache_v)


def _out_mlp_kernel(x_ref, yr_ref, ya_ref, ga_ref, gb_ref, wa_ref, wb_ref, wo_ref, n2_ref, wu_ref, wd_ref,
                    y_ref, *, ff_chunk):
    a = jnp.dot(yr_ref[...], wa_ref[...], preferred_element_type=F32)
    b = jnp.dot(ya_ref[...], wb_ref[...], preferred_element_type=F32)
    mixed = ga_ref[...].astype(F32) * a + gb_ref[...].astype(F32) * b
    h = x_ref[...] + jnp.dot(mixed.astype(BF16), wo_ref[...], preferred_element_type=F32)
    ms = jnp.mean(h * h, axis=-1, keepdims=True)
    hn = (h * lax.rsqrt(ms + EPS) * n2_ref[...]).astype(BF16)
    y = h
    for c0 in range(0, D_FF, ff_chunk):
        u = jnp.maximum(jnp.dot(hn, wu_ref[:, c0:c0 + ff_chunk], preferred_element_type=F32), 0.0)
        y = y + jnp.dot((u * u).astype(BF16), wd_ref[c0:c0 + ff_chunk, :], preferred_element_type=F32)
    y_ref[...] = y


def _out_mlp(x2d, y_rec, y_att, ga, gb, wa16, wb16, wo16, n2, wu16, wd16, tm):
    m = x2d.shape[0]
    row = lambda w: pl.BlockSpec((tm, w), lambda i: (i, 0))
    return pl.pallas_call(
        functools.partial(_out_mlp_kernel, ff_chunk=D_MODEL),
        grid=(m // tm,),
        in_specs=[row(D_MODEL), row(W_REC), row(W_ATT), row(D_MODEL), row(D_MODEL),
                  _const_spec((W_REC, D_MODEL)), _const_spec((W_ATT, D_MODEL)), _const_spec((D_MODEL, D_MODEL)),
                  _const_spec((1, D_MODEL)), _const_spec((D_MODEL, D_FF)), _const_spec((D_FF, D_MODEL))],
        out_specs=row(D_MODEL),
        out_shape=jax.ShapeDtypeStruct((m, D_MODEL), F32),
        compiler_params=pltpu.CompilerParams(dimension_semantics=("parallel",),
                                             vmem_limit_bytes=V7X_VMEM_LIMIT_BYTES),
        name="out_mlp",
    )(x2d, y_rec, y_att, ga, gb, wa16, wb16, wo16, n2, wu16, wd16)


def kernel(x_prompt, x_sample, cache_k, cache_v, state_hgrn, page_table, norm1_w, w_in, lower_bounds, hgrn_norm_w, q_norm_w, k_norm_w, lam_q1, lam_k1, lam_q2, lam_k2, subln_w, rel_bias, w_branch_a, w_branch_b, w_out, norm2_w, w_up, w_down):
    depth = w_in.shape[0]
    assert depth == 1
    bp, sp, _ = x_prompt.shape
    bs, ts, _ = x_sample.shape
    n_pages = page_table.shape[1]
    past = n_pages * PAGE_SIZE
    assert PAGE_SIZE >= MAX_DISTANCE - 1 and ts <= HGRN_CHUNK
    l = 0
    lam_init = 0.8 - 0.6 * math.exp(-0.3 * l)
    scale = DK_ATT ** -0.5

    lb = jnp.cumsum(jax.nn.softmax(lower_bounds.astype(F32), axis=0), axis=0)[l][None, :]
    lam = (jnp.exp(jnp.dot(lam_q1[l].astype(F32), lam_k1[l].astype(F32)))
           - jnp.exp(jnp.dot(lam_q2[l].astype(F32), lam_k2[l].astype(F32))) + lam_init).reshape(1)
    n1 = norm1_w[l].astype(F32)[None, :]
    n2 = norm2_w[l].astype(F32)[None, :]
    qn_row = jnp.tile(q_norm_w[l].astype(F32).reshape(-1), H_ATT)[None, :] * (scale * LOG2E)
    kn_row = jnp.tile(k_norm_w[l].astype(F32).reshape(-1), H_ATT)[None, :]
    hw_row = hgrn_norm_w[l].astype(F32)[None, :]
    sw_row = subln_w[l].astype(F32)[None, :]
    seg = jnp.arange(W_ATT) // DK_ATT
    pseg = jnp.where(seg[:, None] == seg[None, :], 1.0 / DK_ATT, 0.0).astype(BF16)
    w_in16, wa16, wb16 = w_in[l].astype(BF16), w_branch_a[l].astype(BF16), w_branch_b[l].astype(BF16)
    wo16, wu16, wd16 = w_out[l].astype(BF16), w_up[l].astype(BF16), w_down[l].astype(BF16)

    def layer(x, attend, hgrn_fn, tm, kb):
        b, t, _ = x.shape
        x2d = x.reshape(b * t, D_MODEL)
        q, lf, vr, g, qa, k32, k16, v32, vt, ga, gb = _in_proj(x2d, n1, w_in16, lb, qn_row, kn_row, pseg, tm, kb)
        y_rec, s_new = hgrn_fn(q, lf, vr, g)
        y_att = attend(qa, k16, vt)
        y = _out_mlp(x2d, y_rec.reshape(b * t, W_REC), y_att.reshape(b * t, W_ATT), ga, gb,
                     wa16, wb16, wo16, n2, wu16, wd16, tm)
        return (y.reshape(b, t, D_MODEL), k32.reshape(1, b, t, H_ATT, DV_ATT),
                v32.reshape(1, b, t, H_ATT, DV_ATT), s_new[None])

    tq = ATT_BLOCK
    ri = jnp.arange(tq)
    bias_diag = _shifted_bias(rel_bias, ri[None, :] - ri[:, None])
    bias_sub = _shifted_bias(rel_bias, ri[None, :] + tq - ri[:, None])
    r3p = lambda a: a.reshape(bp, sp, a.shape[-1])
    attend_p = lambda qa, k16, vt: _attn_prompt(lam, r3p(qa), r3p(k16), vt, bias_diag, bias_sub, sw_row, tq, lam_init)
    s0p = jnp.zeros((bp, H_REC, DK_REC, DV_REC), F32)
    hgrn_p = lambda q, lf, vr, g: _hgrn(r3p(q), r3p(lf), r3p(vr), r3p(g), s0p, hw_row, 256, HGRN_CHUNK, True)
    y_p, k_p, v_p, s_p = layer(x_prompt, attend_p, hgrn_p, 512, tq)

    half = H_ATT * 2 * ts
    col_h = jnp.arange(half) // (2 * ts)
    col_t = jnp.arange(half) % ts
    assert 2 * half == V7X_LANES and n_pages % 2 == 0

    def own_head(bias):
        out = jnp.zeros(bias.shape[1:], F32)
        for h in range(H_ATT):
            out = jnp.where((col_h == h)[None, :], bias[h], out)
        return out

    keys = jnp.arange(PAGE_SIZE)
    tail_dist = (past + col_t)[None, :] - ((n_pages - 1) * PAGE_SIZE + keys)[:, None]
    zeros_half = jnp.zeros((PAGE_SIZE, half), F32)
    bias_tail = jnp.stack([jnp.zeros((PAGE_SIZE, 2 * half), F32),
                           jnp.concatenate([zeros_half, own_head(_shifted_bias(rel_bias, tail_dist))], axis=1)])
    new_dist = jnp.where((keys < ts)[:, None], col_t[None, :] - keys[:, None], -1)
    bias_new = jnp.concatenate([own_head(_shifted_bias(rel_bias, new_dist)), zeros_half], axis=1)
    ck = cache_k[l].reshape(cache_k.shape[1], PAGE_SIZE * H_ATT, DV_ATT)
    cv = cache_v[l].reshape(cache_v.shape[1], PAGE_SIZE * H_ATT, DV_ATT)

    def attend_s(qa, k16, vt):
        q4 = qa.reshape(bs, ts, H_ATT, DV_ATT).transpose(0, 2, 3, 1)
        d = jnp.arange(DV_ATT)[:, None]
        qcols = jnp.concatenate([jnp.where(d < DK_ATT, q4, 0), jnp.where(d >= DK_ATT, q4, 0)], axis=3)
        eye = jnp.eye(H_ATT, dtype=qcols.dtype)
        wstack = (qcols[:, :, :, None, :] * eye[None, :, None, :, None]).reshape(bs, W_ATT, half)
        zero = jnp.zeros_like(wstack)
        wpair = jnp.concatenate([jnp.concatenate([wstack, zero], axis=2),
                                 jnp.concatenate([zero, wstack], axis=2)], axis=1).astype(BF16)
        return _attn_sample(page_table, lam, wpair, k16.reshape(bs, ts, W_ATT), vt.reshape(bs, ts, W_ATT),
                            bias_new, bias_tail, sw_row, ck, cv, ts, lam_init)

    r3s = lambda a: a.reshape(1, bs * ts, a.shape[-1])
    hgrn_rows = 16 * ts
    hgrn_s = lambda q, lf, vr, g: _hgrn(r3s(q), r3s(lf), r3s(vr), r3s(g), state_hgrn[l].astype(F32), hw_row,
                                        hgrn_rows, ts, False)
    y_s, k_s, v_s, s_s = layer(x_sample, attend_s, hgrn_s, 512, None)
    return (y_p, y_s, k_p, v_p, s_p, k_s, v_s, s_s)
```

```python
import functools
import math

import jax
import jax.numpy as jnp
from jax import lax
from jax.experimental import pallas as pl
from jax.experimental.pallas import tpu as pltpu

F32 = jnp.float32
BF16 = jnp.bfloat16

D_MODEL = 1024
H_REC, DK_REC, DV_REC = 4, 128, 128
W_REC = H_REC * DK_REC
H_ATT, DK_ATT = 4, 64
DV_ATT = 2 * DK_ATT
W_ATT = H_ATT * DV_ATT
D_FF = 4 * D_MODEL
NUM_BUCKETS = 32
MAX_DISTANCE = 128
PAGE_SIZE = 128
HGRN_CHUNK = 32
EPS = 1e-6
NEG = -1e30
LOG2E = math.log2(math.e)

OFF_HQ, OFF_HF, OFF_HI, OFF_HG = 0, W_REC, 2 * W_REC, 3 * W_REC
OFF_AQ = 4 * W_REC
OFF_AK = OFF_AQ + W_ATT
OFF_AV = OFF_AK + W_ATT
OFF_GA = OFF_AV + W_ATT
OFF_GB = OFF_GA + D_MODEL
IN_WIDTH = OFF_GB + D_MODEL

V7X_VMEM_LIMIT_BYTES = 56 * 1024 * 1024
V7X_LANES = 128
PAGES_PER_STEP = 16
PAGE_SLOTS = 3
ATT_BLOCK = 256
ATT_TILE = 128
ATT_HEAD_GROUP = 4

_NT = (((1,), (1,)), ((), ()))


def _const_spec(shape):
    n = len(shape)
    return pl.BlockSpec(shape, lambda *_: (0,) * n, pipeline_mode=pl.Buffered(1))


def _sigmoid(x):
    return 1.0 / (1.0 + jnp.exp(-x))


def _in_proj_kernel(x_ref, n1_ref, w_ref, lb_ref, qn_ref, kn_ref, pseg_ref,
                    q_ref, lf_ref, vr_ref, g_ref, qa_ref, k32_ref, k16_ref, v32_ref, vt_ref,
                    ga_ref, gb_ref):
    tm = x_ref.shape[0]
    x = x_ref[...]
    ms = jnp.mean(x * x, axis=-1, keepdims=True)
    xb = (x * lax.rsqrt(ms + EPS) * n1_ref[...]).astype(BF16)

    def proj(off, size):
        return jnp.dot(xb, w_ref[:, off:off + size], preferred_element_type=F32)

    def seg_rms(z):
        m = jnp.dot((z * z).astype(BF16), pseg_ref[...], preferred_element_type=F32)
        return z * lax.rsqrt(m + EPS)

    def store_head_rows(ref, z):
        for h in range(H_ATT):
            ref[pl.ds(h, tm, stride=H_ATT), :] = z[:, h * DV_ATT:(h + 1) * DV_ATT]

    hq = proj(OFF_HQ, W_REC)
    q_ref[...] = (hq * _sigmoid(hq)).astype(BF16)
    hf = proj(OFF_HF, W_REC)
    lb = lb_ref[...]
    lf_ref[...] = jnp.log(lb + (1.0 - lb) * _sigmoid(hf))
    vr_ref[...] = proj(OFF_HI, W_REC).astype(BF16)
    hg = proj(OFF_HG, W_REC)
    g_ref[...] = (hg * _sigmoid(hg)).astype(BF16)

    qa_ref[...] = (seg_rms(proj(OFF_AQ, W_ATT)) * qn_ref[...]).astype(BF16)
    ka = seg_rms(proj(OFF_AK, W_ATT)) * kn_ref[...]
    store_head_rows(k32_ref, ka)
    k16_ref[...] = ka.astype(BF16)
    va = proj(OFF_AV, W_ATT)
    store_head_rows(v32_ref, va)
    if len(vt_ref.shape) == 2:
        vt_ref[...] = va.astype(BF16)
    else:
        kb = vt_ref.shape[2]
        for i in range(vt_ref.shape[0]):
            vt_ref[i] = va[i * kb:(i + 1) * kb, :].T.astype(BF16)
    ga_ref[...] = _sigmoid(proj(OFF_GA, D_MODEL)).astype(BF16)
    gb_ref[...] = _sigmoid(proj(OFF_GB, D_MODEL)).astype(BF16)


def _in_proj(x2d, n1, w_in16, lb, qn_row, kn_row, pseg, tm, kb):
    m = x2d.shape[0]
    row = lambda w: pl.BlockSpec((tm, w), lambda i: (i, 0))
    head_rows = pl.BlockSpec((tm * H_ATT, DV_ATT), lambda i: (i, 0))
    if kb is None:
        vt_spec, vt_shape = row(W_ATT), (m, W_ATT)
    else:
        vt_spec, vt_shape = pl.BlockSpec((tm // kb, W_ATT, kb), lambda i: (i, 0, 0)), (m // kb, W_ATT, kb)
    sds = jax.ShapeDtypeStruct
    return pl.pallas_call(
        _in_proj_kernel,
        grid=(m // tm,),
        in_specs=[row(D_MODEL), _const_spec((1, D_MODEL)), _const_spec((D_MODEL, IN_WIDTH)),
                  _const_spec((1, W_REC)), _const_spec((1, W_ATT)), _const_spec((1, W_ATT)),
                  _const_spec((W_ATT, W_ATT))],
        out_specs=[row(W_REC), row(W_REC), row(W_REC), row(W_REC), row(W_ATT), head_rows, row(W_ATT),
                   head_rows, vt_spec, row(D_MODEL), row(D_MODEL)],
        out_shape=[sds((m, W_REC), BF16), sds((m, W_REC), F32), sds((m, W_REC), BF16), sds((m, W_REC), BF16),
                   sds((m, W_ATT), BF16), sds((m * H_ATT, DV_ATT), F32), sds((m, W_ATT), BF16),
                   sds((m * H_ATT, DV_ATT), F32), sds(vt_shape, BF16),
                   sds((m, D_MODEL), BF16), sds((m, D_MODEL), BF16)],
        compiler_params=pltpu.CompilerParams(dimension_semantics=("parallel",),
                                             vmem_limit_bytes=V7X_VMEM_LIMIT_BYTES),
        name="in_proj",
    )(x2d, n1, w_in16, lb, qn_row, kn_row, pseg)


def _hgrn_kernel(q_ref, lf_ref, v_ref, g_ref, s0_ref, w_ref, y_ref, s_ref, st_scr, *, c, carry):
    r = q_ref.shape[1]
    nc = r // c
    t = pl.program_id(1)
    mm = BF16 if c >= 16 else F32
    shift = c.bit_length() - 1
    r_i = lax.broadcasted_iota(jnp.int32, (r, r), 0)
    c_i = lax.broadcasted_iota(jnp.int32, (r, r), 1)
    tril = ((r_i >> shift) == (c_i >> shift)) & (r_i >= c_i)
    tril16 = jnp.where(tril, 1.0, 0.0).astype(BF16)
    lane_chunk = lax.broadcasted_iota(jnp.int32, (DK_REC, r), 1) >> shift

    def rows_to_cols(row):
        return jnp.broadcast_to(row, (row.shape[1], row.shape[1])).T

    if carry:
        @pl.when(t == 0)
        def _():
            for h in range(H_REC):
                st_scr[h] = s0_ref[0, h]

    heads = range(H_REC)
    cols = [slice(h * DK_REC, (h + 1) * DK_REC) for h in heads]
    lf = [lf_ref[0, :, cols[h]] for h in heads]
    v16 = [v_ref[0, :, cols[h]] for h in heads]
    b = []
    for h in heads:
        hi = lf[h].astype(BF16)
        lo = (lf[h] - hi.astype(F32)).astype(BF16)
        b.append(jnp.dot(tril16, hi, preferred_element_type=F32) + jnp.dot(tril16, lo, preferred_element_type=F32))
    qe, qe16, ke16, kd, dec = [], [], [], [], []
    for h in heads:
        b3 = b[h].reshape(nc, c, DK_REC)
        bl = b3[:, c - 1:c, :]
        k = 1.0 - jnp.exp(lf[h])
        qe.append(q_ref[0, :, cols[h]].astype(F32) * jnp.exp(b[h]))
        qe16.append(qe[h].astype(BF16))
        ke16.append((k * jnp.exp(-b[h])).astype(BF16))
        kd.append((k.reshape(nc, c, DK_REC) * jnp.exp(bl - b3)).reshape(r, DK_REC))
        dec.append(jnp.exp(bl))
    a = [lax.dot_general(qe16[h], ke16[h], _NT, preferred_element_type=F32) for h in heads]
    u = []
    for h in heads:
        kdt = kd[h].T
        lhs = jnp.concatenate([jnp.where(lane_chunk == n, kdt, 0.0) for n in range(nc)], axis=0).astype(BF16)
        u.append(jnp.dot(lhs, v16[h], preferred_element_type=F32))
    o = [jnp.dot(jnp.where(tril, a[h], 0.0).astype(BF16), v16[h], preferred_element_type=F32) for h in heads]
    qe_mm = qe16 if mm == BF16 else qe
    parts = [[] for _ in heads]
    st = [st_scr[h] for h in heads] if carry else None
    for n in range(nc):
        rows = slice(n * c, (n + 1) * c)
        for h in heads:
            s_in = st[h] if carry else s0_ref[n, h]
            parts[h].append(jnp.dot(qe_mm[h][rows], s_in.astype(mm), preferred_element_type=F32))
            s_out = s_in * rows_to_cols(dec[h][n]) + u[h][n * DK_REC:(n + 1) * DK_REC]
            if carry:
                st[h] = s_out
            else:
                s_ref[n, h] = s_out
    for h in heads:
        if carry:
            st_scr[h] = st[h]
        oh = o[h] + jnp.concatenate(parts[h], axis=0)
        on = oh * lax.rsqrt(jnp.mean(oh * oh, axis=-1, keepdims=True) + EPS) * w_ref[...]
        y_ref[0, :, cols[h]] = (on * g_ref[0, :, cols[h]].astype(F32)).astype(BF16)

    if carry:
        @pl.when(t == pl.num_programs(1) - 1)
        def _():
            for h in range(H_REC):
                s_ref[0, h] = st_scr[h]


def _hgrn(q, lf, v, g, s0, w_row, rows, c, carry):
    gdim, t, _ = q.shape
    nseq = 1 if carry else rows // c
    blk = pl.BlockSpec((1, rows, W_REC), lambda gi, ti: (gi, ti, 0))
    if carry:
        sblk = pl.BlockSpec((1, H_REC, DK_REC, DV_REC), lambda gi, ti: (gi, 0, 0, 0))
    else:
        sblk = pl.BlockSpec((nseq, H_REC, DK_REC, DV_REC), lambda gi, ti: (ti, 0, 0, 0))
    return pl.pallas_call(
        functools.partial(_hgrn_kernel, c=c, carry=carry),
        grid=(gdim, t // rows),
        in_specs=[blk, blk, blk, blk, sblk, _const_spec((1, DV_REC))],
        out_specs=[blk, sblk],
        out_shape=[jax.ShapeDtypeStruct((gdim, t, W_REC), BF16), jax.ShapeDtypeStruct(s0.shape, F32)],
        scratch_shapes=[pltpu.VMEM((H_REC, DK_REC, DV_REC), F32)],
        compiler_params=pltpu.CompilerParams(dimension_semantics=("parallel", "arbitrary"),
                                             vmem_limit_bytes=V7X_VMEM_LIMIT_BYTES),
        name="hgrn",
    )(q, lf, v, g, s0, w_row)


def _rel_bucket(dist):
    dist = jnp.maximum(dist, 0)
    max_exact = NUM_BUCKETS // 2
    large = max_exact + (jnp.log(jnp.maximum(dist, 1).astype(F32) / max_exact)
                         / math.log(MAX_DISTANCE / max_exact) * (NUM_BUCKETS - max_exact)).astype(jnp.int32)
    large = jnp.minimum(large, NUM_BUCKETS - 1)
    return jnp.where(dist < max_exact, dist, large)


def _shifted_bias(rel_bias, dist):
    rb = rel_bias.astype(F32)
    rb = (rb - rb[NUM_BUCKETS - 1]) * LOG2E
    bucket = _rel_bucket(dist)[None]
    expand = (slice(None),) + (None,) * dist.ndim
    out = jnp.zeros((rb.shape[1],) + dist.shape, F32)
    for kb in range(NUM_BUCKETS - 1):
        out = jnp.where(bucket == kb, rb[kb][expand], out)
    return jnp.where((dist >= 0)[None], out, NEG)


def _split_halves(q):
    lane = lax.broadcasted_iota(jnp.int32, q.shape, 1)
    zero = jnp.zeros_like(q)
    return jnp.concatenate([jnp.where(lane < DK_ATT, q, zero), jnp.where(lane >= DK_ATT, q, zero)], axis=0)


def _attn_prompt_kernel(lam_ref, q_ref, k_ref, vt_ref, bd_ref, bs_ref, w_ref, o_ref, *, tq, lam_init):
    qi = pl.program_id(1)
    for h0 in range(0, H_ATT, ATT_HEAD_GROUP):
        _attn_prompt_heads(range(h0, h0 + ATT_HEAD_GROUP), qi, lam_ref, q_ref, k_ref, vt_ref, bd_ref, bs_ref,
                           w_ref, o_ref, tq, lam_init)


def _attn_prompt_heads(heads, qi, lam_ref, q_ref, k_ref, vt_ref, bd_ref, bs_ref, w_ref, o_ref, tq, lam_init):
    nt = 2 * tq // ATT_TILE
    cols = {h: slice(h * DV_ATT, (h + 1) * DV_ATT) for h in heads}
    q_tiles = {}
    for h in heads:
        qs = _split_halves(q_ref[0, :, cols[h]])
        q_tiles[h] = [qs[i * ATT_TILE:(i + 1) * ATT_TILE] for i in range(nt)]

    def scores(j):
        out = []
        for h in heads:
            k_blk = k_ref[0, pl.ds(pl.multiple_of(j * tq, tq), tq), cols[h]]
            out.append(tuple(lax.dot_general(k_blk, q_tiles[h][i], _NT, preferred_element_type=F32)
                             for i in range(nt)))
        return tuple(out)

    def update(j, s_all, bias_ref, extra, carry):
        out = []
        for g, h in enumerate(heads):
            vt_blk = vt_ref[j, cols[h], :]
            tiles = []
            for i in range(nt):
                m, l, acc = carry[g][i]
                s = s_all[g][i]
                if bias_ref is not None:
                    c0 = (i * ATT_TILE) % tq
                    s = s + bias_ref[h, :, c0:c0 + ATT_TILE]
                    if extra is not None:
                        s = s + extra
                m_new = jnp.maximum(m, jnp.max(s, axis=0, keepdims=True))
                alpha = jnp.exp2(m - m_new)
                p = jnp.exp2(s - m_new)
                l = l * alpha + jnp.sum(p, axis=0, keepdims=True)
                acc = acc * alpha + jnp.dot(vt_blk, p.astype(BF16), preferred_element_type=F32)
                tiles.append((m_new, l, acc))
            out.append(tuple(tiles))
        return tuple(out)

    init = tuple(tuple((jnp.full((1, ATT_TILE), NEG, F32), jnp.zeros((1, ATT_TILE), F32),
                        jnp.zeros((DV_ATT, ATT_TILE), F32)) for _ in range(nt)) for _ in heads)
    sub_j = jnp.maximum(qi - 1, 0)
    n_far = jnp.maximum(qi - 1, 0)
    s_diag = scores(qi)
    s_sub = scores(sub_j)
    carry = update(qi, s_diag, bd_ref, None, init)
    carry = update(sub_j, s_sub, bs_ref, jnp.where(qi == 0, NEG, 0.0), carry)

    def far_step(j, cr):
        return update(j, scores(j), None, None, cr)

    carry = lax.fori_loop(0, n_far, far_step, carry)
    for g, h in enumerate(heads):
        o_t = jnp.concatenate([acc / l for _, l, acc in carry[g]], axis=1)
        o_t = o_t[:, :tq] - lam_ref[0] * o_t[:, tq:]
        o_t = o_t * lax.rsqrt(jnp.mean(o_t * o_t, axis=0, keepdims=True) + EPS)
        o_ref[0, :, cols[h]] = (o_t.T * w_ref[...] * (1.0 - lam_init)).astype(BF16)


def _attn_prompt(lam, qa, k16, vt, bias_diag, bias_sub, w_row, tq, lam_init):
    b, s, _ = qa.shape
    nb = s // tq
    assert tq >= MAX_DISTANCE and s % tq == 0
    return pl.pallas_call(
        functools.partial(_attn_prompt_kernel, tq=tq, lam_init=lam_init),
        grid=(b, nb),
        in_specs=[pl.BlockSpec(memory_space=pltpu.SMEM),
                  pl.BlockSpec((1, tq, W_ATT), lambda bi, qi: (bi, qi, 0)),
                  pl.BlockSpec((1, s, W_ATT), lambda bi, qi: (bi, 0, 0)),
                  pl.BlockSpec((nb, W_ATT, tq), lambda bi, qi: (bi, 0, 0)),
                  _const_spec((H_ATT, tq, tq)), _const_spec((H_ATT, tq, tq)), _const_spec((1, DV_ATT))],
        out_specs=pl.BlockSpec((1, tq, W_ATT), lambda bi, qi: (bi, qi, 0)),
        out_shape=jax.ShapeDtypeStruct((b, s, W_ATT), BF16),
        compiler_params=pltpu.CompilerParams(dimension_semantics=("parallel", "arbitrary")),
        name="attn_prompt",
    )(lam, qa, k16, vt, bias_diag, bias_sub, w_row)


def _attn_sample_kernel(pt_ref, lam_ref, wp_ref, kn_ref, vn_ref, bnew_ref, btail_ref, w_ref, ck_hbm, cv_hbm,
                        o_ref, kbuf, vbuf, sem, m_scr, l_scr, acc_scr, *, t_new, lam_init):
    npg = PAGES_PER_STEP
    ahead = PAGE_SLOTS - 1
    j = pl.program_id(1)
    n_steps = pl.num_programs(1)
    g = pl.program_id(0) * n_steps + j
    last = pl.num_programs(0) * n_steps - 1
    slot = lax.rem(g, PAGE_SLOTS)
    rh = 2 * t_new
    half = H_ATT * rh
    ncol = 2 * half
    q_cols = wp_ref[0]
    lane = lax.broadcasted_iota(jnp.int32, q_cols.shape, 1)
    wpair = jnp.concatenate(
        [jnp.where((lane >= par * half + h * rh) & (lane < par * half + (h + 1) * rh), q_cols, jnp.zeros_like(q_cols))
         for par in range(2) for h in range(H_ATT)], axis=0)

    def page_copies(step, slot_):
        row, col0 = lax.div(step, n_steps), lax.rem(step, n_steps) * npg
        copies = []
        for i in range(npg):
            page = pt_ref[row, col0 + i]
            copies.append(pltpu.make_async_copy(ck_hbm.at[page], kbuf.at[slot_, i], sem.at[slot_]))
            copies.append(pltpu.make_async_copy(cv_hbm.at[page], vbuf.at[slot_, i], sem.at[slot_]))
        return copies

    def later_step(d):
        return jnp.minimum(g + d, last), lax.rem(g + d, PAGE_SLOTS)

    @pl.when(g == 0)
    def _():
        for d in range(ahead):
            for c in page_copies(*later_step(d)):
                c.start()

    for c in page_copies(g, slot):
        c.wait()
    k_refs = [kbuf.at[slot, i] for i in range(npg)]
    v_refs = [vbuf.at[slot, i] for i in range(npg)]

    def head_rows(ref, h):
        return ref[pl.ds(h, PAGE_SIZE, stride=H_ATT), :].astype(BF16)

    def rows_to_cols(row):
        return jnp.broadcast_to(row, (row.shape[1], row.shape[1])).T

    def weighted_values(p_t, get_v, parity):
        return [jnp.dot(p_t[parity * half + h * rh:parity * half + (h + 1) * rh], get_v(h),
                        preferred_element_type=F32) for h in range(H_ATT)]

    @pl.when(j == 0)
    def _():
        pad = PAGE_SIZE - t_new
        k_new = jnp.concatenate([kn_ref[0], jnp.zeros((pad, W_ATT), BF16)], axis=0)
        v_new = jnp.concatenate([vn_ref[0], jnp.zeros((pad, W_ATT), BF16)], axis=0)
        s = jnp.dot(k_new, wpair[:W_ATT, :], preferred_element_type=F32) + bnew_ref[...]
        m = jnp.max(s, axis=0, keepdims=True)
        even = lax.broadcasted_iota(jnp.int32, s.shape, 1) < half
        p = jnp.where(even, jnp.exp2(s - m), 0.0)
        m_scr[...] = jnp.where(even[:1], m, NEG)
        l_scr[...] = jnp.sum(p, axis=0, keepdims=True)
        p_t = p.T.astype(BF16)
        acc_scr[...] = jnp.concatenate(
            weighted_values(p_t, lambda h: v_new[:, h * DV_ATT:(h + 1) * DV_ATT], 0)
            + [jnp.zeros((half, DV_ATT), F32)], axis=0)

    for c in page_copies(*later_step(ahead)):
        c.start()
    tiles = []
    for pi in range(npg // 2):
        lhs = jnp.concatenate([head_rows(k_refs[2 * pi + par], h) for par in range(2) for h in range(H_ATT)],
                              axis=1)
        s = jnp.dot(lhs, wpair, preferred_element_type=F32)
        if pi == npg // 2 - 1:
            s = s + btail_ref[0]
        tiles.append(s)
    mx = tiles[0]
    for s in tiles[1:]:
        mx = jnp.maximum(mx, s)
    m_old = m_scr[...]
    m_new = jnp.maximum(m_old, jnp.max(mx, axis=0, keepdims=True))
    alpha = jnp.exp2(m_old - m_new)
    acc = acc_scr[...] * rows_to_cols(alpha)
    lsum = jnp.zeros_like(alpha)
    for pi in range(npg // 2):
        p = jnp.exp2(tiles[pi] - m_new)
        lsum = lsum + jnp.sum(p, axis=0, keepdims=True)
        p_t = p.T.astype(BF16)
        acc = acc + jnp.concatenate(
            [pv for par in range(2)
             for pv in weighted_values(p_t, functools.partial(head_rows, v_refs[2 * pi + par]), par)], axis=0)
    l = l_scr[...] * alpha + lsum
    m_scr[...] = m_new
    l_scr[...] = l
    acc_scr[...] = acc

    @pl.when(j == pl.num_programs(1) - 1)
    def _():
        m_c, l_c = rows_to_cols(m_new), rows_to_cols(l)
        m_f = jnp.maximum(m_c[:half], m_c[half:])
        w_e, w_o = jnp.exp2(m_c[:half] - m_f), jnp.exp2(m_c[half:] - m_f)
        o = (acc[:half] * w_e + acc[half:] * w_o) / (l_c[:half] * w_e + l_c[half:] * w_o)
        for h in range(H_ATT):
            oh = o[h * rh:h * rh + t_new] - lam_ref[0] * o[h * rh + t_new:(h + 1) * rh]
            oh = oh * lax.rsqrt(jnp.mean(oh * oh, axis=-1, keepdims=True) + EPS) * w_ref[...] * (1.0 - lam_init)
            o_ref[0, :, h * DV_ATT:(h + 1) * DV_ATT] = oh.astype(BF16)

    @pl.when(g == last)
    def _():
        for d in range(1, PAGE_SLOTS):
            for c in page_copies(*later_step(d)):
                c.wait()


def _attn_sample(page_table, lam, wpair, k_new, v_new, bias_new, bias_tail, w_row, cache_k, cache_v,
                 t_new, lam_init):
    nb, n_pages = page_table.shape
    npg = PAGES_PER_STEP
    assert n_pages % npg == 0 and npg % 2 == 0
    n_steps = n_pages // npg
    ncol = wpair.shape[2]
    prow = PAGE_SIZE * H_ATT
    per_b = lambda n, w: pl.BlockSpec((1, n, w), lambda b, j, pt: (b, 0, 0))
    const = lambda shape: pl.BlockSpec(shape, lambda b, j, pt: (0,) * len(shape))
    hbm = pl.BlockSpec(memory_space=pl.ANY)
    grid_spec = pltpu.PrefetchScalarGridSpec(
        num_scalar_prefetch=1,
        grid=(nb, n_steps),
        in_specs=[pl.BlockSpec(memory_space=pltpu.SMEM), per_b(DV_ATT, ncol), per_b(t_new, W_ATT),
                  per_b(t_new, W_ATT), const((PAGE_SIZE, ncol)),
                  pl.BlockSpec((1, PAGE_SIZE, ncol), lambda b, j, pt: (jnp.where(j == n_steps - 1, 1, 0), 0, 0)),
                  const((1, DV_ATT)), hbm, hbm],
        out_specs=pl.BlockSpec((1, t_new, W_ATT), lambda b, j, pt: (b, 0, 0)),
        scratch_shapes=[pltpu.VMEM((PAGE_SLOTS, npg, prow, DV_ATT), F32),
                        pltpu.VMEM((PAGE_SLOTS, npg, prow, DV_ATT), F32),
                        pltpu.SemaphoreType.DMA((PAGE_SLOTS,)),
                        pltpu.VMEM((1, ncol), F32), pltpu.VMEM((1, ncol), F32), pltpu.VMEM((ncol, DV_ATT), F32)],
    )
    return pl.pallas_call(
        functools.partial(_attn_sample_kernel, t_new=t_new, lam_init=lam_init),
        grid_spec=grid_spec,
        out_shape=jax.ShapeDtypeStruct((nb, t_new, W_ATT), BF16),
        compiler_params=pltpu.CompilerParams(dimension_semantics=("arbitrary", "arbitrary"),
                                             vmem_limit_bytes=V7X_VMEM_LIMIT_BYTES),
        name="attn_sample",
    )(page_table, lam, wpair, k_new, v_new, bias_new, bias_tail, w_row, cache_k, cache_v)


def _out_mlp_kernel(x_ref, yr_ref, ya_ref, ga_ref, gb_ref, wa_ref, wb_ref, wo_ref, n2_ref, wu_ref, wd_ref,
                    y_ref, *, ff_chunk):
    a = jnp.dot(yr_ref[...], wa_ref[...], preferred_element_type=F32)
    b = jnp.dot(ya_ref[...], wb_ref[...], preferred_element_type=F32)
    mixed = ga_ref[...].astype(F32) * a + gb_ref[...].astype(F32) * b
    h = x_ref[...] + jnp.dot(mixed.astype(BF16), wo_ref[...], preferred_element_type=F32)
    ms = jnp.mean(h * h, axis=-1, keepdims=True)
    hn = (h * lax.rsqrt(ms + EPS) * n2_ref[...]).astype(BF16)
    y = h
    for c0 in range(0, D_FF, ff_chunk):
        u = jnp.maximum(jnp.dot(hn, wu_ref[:, c0:c0 + ff_chunk], preferred_element_type=F32), 0.0)
        y = y + jnp.dot((u * u).astype(BF16), wd_ref[c0:c0 + ff_chunk, :], preferred_element_type=F32)
    y_ref[...] = y


def _out_mlp(x2d, y_rec, y_att, ga, gb, wa16, wb16, wo16, n2, wu16, wd16, tm):
    m = x2d.shape[0]
    row = lambda w: pl.BlockSpec((tm, w), lambda i: (i, 0))
    return pl.pallas_call(
        functools.partial(_out_mlp_kernel, ff_chunk=D_MODEL),
        grid=(m // tm,),
        in_specs=[row(D_MODEL), row(W_REC), row(W_ATT), row(D_MODEL), row(D_MODEL),
                  _const_spec((W_REC, D_MODEL)), _const_spec((W_ATT, D_MODEL)), _const_spec((D_MODEL, D_MODEL)),
                  _const_spec((1, D_MODEL)), _const_spec((D_MODEL, D_FF)), _const_spec((D_FF, D_MODEL))],
        out_specs=row(D_MODEL),
        out_shape=jax.ShapeDtypeStruct((m, D_MODEL), F32),
        compiler_params=pltpu.CompilerParams(dimension_semantics=("parallel",),
                                             vmem_limit_bytes=V7X_VMEM_LIMIT_BYTES),
        name="out_mlp",
    )(x2d, y_rec, y_att, ga, gb, wa16, wb16, wo16, n2, wu16, wd16)


def kernel(x_prompt, x_sample, cache_k, cache_v, state_hgrn, page_table, norm1_w, w_in, lower_bounds, hgrn_norm_w, q_norm_w, k_norm_w, lam_q1, lam_k1, lam_q2, lam_k2, subln_w, rel_bias, w_branch_a, w_branch_b, w_out, norm2_w, w_up, w_down):
    depth = w_in.shape[0]
    assert depth == 1
    bp, sp, _ = x_prompt.shape
    bs, ts, _ = x_sample.shape
    n_pages = page_table.shape[1]
    past = n_pages * PAGE_SIZE
    assert PAGE_SIZE >= MAX_DISTANCE - 1 and ts <= HGRN_CHUNK
    l = 0
    lam_init = 0.8 - 0.6 * math.exp(-0.3 * l)
    scale = DK_ATT ** -0.5

    lb = jnp.cumsum(jax.nn.softmax(lower_bounds.astype(F32), axis=0), axis=0)[l][None, :]
    lam = (jnp.exp(jnp.dot(lam_q1[l].astype(F32), lam_k1[l].astype(F32)))
           - jnp.exp(jnp.dot(lam_q2[l].astype(F32), lam_k2[l].astype(F32))) + lam_init).reshape(1)
    n1 = norm1_w[l].astype(F32)[None, :]
    n2 = norm2_w[l].astype(F32)[None, :]
    qn_row = jnp.tile(q_norm_w[l].astype(F32).reshape(-1), H_ATT)[None, :] * (scale * LOG2E)
    kn_row = jnp.tile(k_norm_w[l].astype(F32).reshape(-1), H_ATT)[None, :]
    hw_row = hgrn_norm_w[l].astype(F32)[None, :]
    sw_row = subln_w[l].astype(F32)[None, :]
    seg = jnp.arange(W_ATT) // DK_ATT
    pseg = jnp.where(seg[:, None] == seg[None, :], 1.0 / DK_ATT, 0.0).astype(BF16)
    w_in16, wa16, wb16 = w_in[l].astype(BF16), w_branch_a[l].astype(BF16), w_branch_b[l].astype(BF16)
    wo16, wu16, wd16 = w_out[l].astype(BF16), w_up[l].astype(BF16), w_down[l].astype(BF16)

    def layer(x, attend, hgrn_fn, tm, kb):
        b, t, _ = x.shape
        x2d = x.reshape(b * t, D_MODEL)
        q, lf, vr, g, qa, k32, k16, v32, vt, ga, gb = _in_proj(x2d, n1, w_in16, lb, qn_row, kn_row, pseg, tm, kb)
        y_rec, s_new = hgrn_fn(q, lf, vr, g)
        y_att = attend(qa, k16, vt)
        y = _out_mlp(x2d, y_rec.reshape(b * t, W_REC), y_att.reshape(b * t, W_ATT), ga, gb,
                     wa16, wb16, wo16, n2, wu16, wd16, tm)
        return (y.reshape(b, t, D_MODEL), k32.reshape(1, b, t, H_ATT, DV_ATT),
                v32.reshape(1, b, t, H_ATT, DV_ATT), s_new[None])

    tq = ATT_BLOCK
    ri = jnp.arange(tq)
    bias_diag = _shifted_bias(rel_bias, ri[None, :] - ri[:, None])
    bias_sub = _shifted_bias(rel_bias, ri[None, :] + tq - ri[:, None])
    r3p = lambda a: a.reshape(bp, sp, a.shape[-1])
    attend_p = lambda qa, k16, vt: _attn_prompt(lam, r3p(qa), r3p(k16), vt, bias_diag, bias_sub, sw_row, tq, lam_init)
    s0p = jnp.zeros((bp, H_REC, DK_REC, DV_REC), F32)
    hgrn_p = lambda q, lf, vr, g: _hgrn(r3p(q), r3p(lf), r3p(vr), r3p(g), s0p, hw_row, 256, HGRN_CHUNK, True)
    y_p, k_p, v_p, s_p = layer(x_prompt, attend_p, hgrn_p, 512, tq)

    half = H_ATT * 2 * ts
    col_h = jnp.arange(half) // (2 * ts)
    col_t = jnp.arange(half) % ts
    assert 2 * half == V7X_LANES and n_pages % 2 == 0

    def own_head(bias):
        out = jnp.zeros(bias.shape[1:], F32)
        for h in range(H_ATT):
            out = jnp.where((col_h == h)[None, :], bias[h], out)
        return out

    keys = jnp.arange(PAGE_SIZE)
    tail_dist = (past + col_t)[None, :] - ((n_pages - 1) * PAGE_SIZE + keys)[:, None]
    zeros_half = jnp.zeros((PAGE_SIZE, half), F32)
    bias_tail = jnp.stack([jnp.zeros((PAGE_SIZE, 2 * half), F32),
                           jnp.concatenate([zeros_half, own_head(_shifted_bias(rel_bias, tail_dist))], axis=1)])
    new_dist = jnp.where((keys < ts)[:, None], col_t[None, :] - keys[:, None], -1)
    bias_new = jnp.concatenate([own_head(_shifted_bias(rel_bias, new_dist)), zeros_half], axis=1)
    ck = cache_k[l].reshape(cache_k.shape[1], PAGE_SIZE * H_ATT, DV_ATT)
    cv = cache_v[l].reshape(cache_v.shape[1], PAGE_SIZE * H_ATT, DV_ATT)

    def attend_s(qa, k16, vt):
        q4 = qa.reshape(bs, ts, H_ATT, DV_ATT).transpose(0, 3, 2, 1)
        d = jnp.arange(DV_ATT)[:, None, None]
        qcols = jnp.concatenate([jnp.where(d < DK_ATT, q4, 0), jnp.where(d >= DK_ATT, q4, 0)], axis=3)
        qcols = qcols.reshape(bs, DV_ATT, half)
        return _attn_sample(page_table, lam, jnp.concatenate([qcols, qcols], axis=2).astype(BF16),
                            k16.reshape(bs, ts, W_ATT), vt.reshape(bs, ts, W_ATT),
                            bias_new, bias_tail, sw_row, ck, cv, ts, lam_init)

    r3s = lambda a: a.reshape(1, bs * ts, a.shape[-1])
    hgrn_rows = 16 * ts
    hgrn_s = lambda q, lf, vr, g: _hgrn(r3s(q), r3s(lf), r3s(vr), r3s(g), state_hgrn[l].astype(F32), hw_row,
                                        hgrn_rows, ts, False)
    y_s, k_s, v_s, s_s = layer(x_sample, attend_s, hgrn_s, 512, None)
    return (y_p, y_s, k_p, v_p, s_p, k_s, v_s, s_s)
```

```python
import functools
import math

import jax
import jax.numpy as jnp
from jax import lax
from jax.experimental import pallas as pl
from jax.experimental.pallas import tpu as pltpu

F32 = jnp.float32
BF16 = jnp.bfloat16

D_MODEL = 1024
H_REC, DK_REC, DV_REC = 4, 128, 128
W_REC = H_REC * DK_REC
H_ATT, DK_ATT = 4, 64
DV_ATT = 2 * DK_ATT
W_ATT = H_ATT * DV_ATT
D_FF = 4 * D_MODEL
NUM_BUCKETS = 32
MAX_DISTANCE = 128
PAGE_SIZE = 128
HGRN_CHUNK = 32
EPS = 1e-6
NEG = -1e30
LOG2E = math.log2(math.e)

OFF_HQ, OFF_HF, OFF_HI, OFF_HG = 0, W_REC, 2 * W_REC, 3 * W_REC
OFF_AQ = 4 * W_REC
OFF_AK = OFF_AQ + W_ATT
OFF_AV = OFF_AK + W_ATT
OFF_GA = OFF_AV + W_ATT
OFF_GB = OFF_GA + D_MODEL
IN_WIDTH = OFF_GB + D_MODEL

V7X_VMEM_LIMIT_BYTES = 56 * 1024 * 1024
V7X_LANES = 128
PAGES_PER_STEP = 16
PAGE_SLOTS = 3
ATT_BLOCK = 256
ATT_TILE = 128
ATT_HEAD_GROUP = 4

_NT = (((1,), (1,)), ((), ()))


def _const_spec(shape):
    n = len(shape)
    return pl.BlockSpec(shape, lambda *_: (0,) * n, pipeline_mode=pl.Buffered(1))


def _sigmoid(x):
    return 1.0 / (1.0 + jnp.exp(-x))


def _in_proj_kernel(x_ref, n1_ref, w_ref, lb_ref, qn_ref, kn_ref, pseg_ref,
                    q_ref, lf_ref, vr_ref, g_ref, qa_ref, k32_ref, k16_ref, v32_ref, vt_ref,
                    ga_ref, gb_ref):
    tm = x_ref.shape[0]
    x = x_ref[...]
    ms = jnp.mean(x * x, axis=-1, keepdims=True)
    xb = (x * lax.rsqrt(ms + EPS) * n1_ref[...]).astype(BF16)

    def proj(off, size):
        return jnp.dot(xb, w_ref[:, off:off + size], preferred_element_type=F32)

    def seg_rms(z):
        m = jnp.dot((z * z).astype(BF16), pseg_ref[...], preferred_element_type=F32)
        return z * lax.rsqrt(m + EPS)

    def store_head_rows(ref, z):
        for h in range(H_ATT):
            ref[pl.ds(h, tm, stride=H_ATT), :] = z[:, h * DV_ATT:(h + 1) * DV_ATT]

    hq = proj(OFF_HQ, W_REC)
    q_ref[...] = (hq * _sigmoid(hq)).astype(BF16)
    hf = proj(OFF_HF, W_REC)
    lb = lb_ref[...]
    lf_ref[...] = jnp.log(lb + (1.0 - lb) * _sigmoid(hf))
    vr_ref[...] = proj(OFF_HI, W_REC).astype(BF16)
    hg = proj(OFF_HG, W_REC)
    g_ref[...] = (hg * _sigmoid(hg)).astype(BF16)

    qa_ref[...] = (seg_rms(proj(OFF_AQ, W_ATT)) * qn_ref[...]).astype(BF16)
    ka = seg_rms(proj(OFF_AK, W_ATT)) * kn_ref[...]
    store_head_rows(k32_ref, ka)
    k16_ref[...] = ka.astype(BF16)
    va = proj(OFF_AV, W_ATT)
    store_head_rows(v32_ref, va)
    if len(vt_ref.shape) == 2:
        vt_ref[...] = va.astype(BF16)
    else:
        kb = vt_ref.shape[2]
        for i in range(vt_ref.shape[0]):
            vt_ref[i] = va[i * kb:(i + 1) * kb, :].T.astype(BF16)
    ga_ref[...] = _sigmoid(proj(OFF_GA, D_MODEL)).astype(BF16)
    gb_ref[...] = _sigmoid(proj(OFF_GB, D_MODEL)).astype(BF16)


def _in_proj(x2d, n1, w_in16, lb, qn_row, kn_row, pseg, tm, kb):
    m = x2d.shape[0]
    row = lambda w: pl.BlockSpec((tm, w), lambda i: (i, 0))
    head_rows = pl.BlockSpec((tm * H_ATT, DV_ATT), lambda i: (i, 0))
    if kb is None:
        vt_spec, vt_shape = row(W_ATT), (m, W_ATT)
    else:
        vt_spec, vt_shape = pl.BlockSpec((tm // kb, W_ATT, kb), lambda i: (i, 0, 0)), (m // kb, W_ATT, kb)
    sds = jax.ShapeDtypeStruct
    return pl.pallas_call(
        _in_proj_kernel,
        grid=(m // tm,),
        in_specs=[row(D_MODEL), _const_spec((1, D_MODEL)), _const_spec((D_MODEL, IN_WIDTH)),
                  _const_spec((1, W_REC)), _const_spec((1, W_ATT)), _const_spec((1, W_ATT)),
                  _const_spec((W_ATT, W_ATT))],
        out_specs=[row(W_REC), row(W_REC), row(W_REC), row(W_REC), row(W_ATT), head_rows, row(W_ATT),
                   head_rows, vt_spec, row(D_MODEL), row(D_MODEL)],
        out_shape=[sds((m, W_REC), BF16), sds((m, W_REC), F32), sds((m, W_REC), BF16), sds((m, W_REC), BF16),
                   sds((m, W_ATT), BF16), sds((m * H_ATT, DV_ATT), F32), sds((m, W_ATT), BF16),
                   sds((m * H_ATT, DV_ATT), F32), sds(vt_shape, BF16),
                   sds((m, D_MODEL), BF16), sds((m, D_MODEL), BF16)],
        compiler_params=pltpu.CompilerParams(dimension_semantics=("parallel",),
                                             vmem_limit_bytes=V7X_VMEM_LIMIT_BYTES),
        name="in_proj",
    )(x2d, n1, w_in16, lb, qn_row, kn_row, pseg)


def _hgrn_kernel(q_ref, lf_ref, v_ref, g_ref, s0_ref, w_ref, y_ref, s_ref, st_scr, *, c, carry):
    r = q_ref.shape[1]
    nc = r // c
    t = pl.program_id(1)
    mm = BF16 if c >= 16 else F32
    shift = c.bit_length() - 1
    r_i = lax.broadcasted_iota(jnp.int32, (r, r), 0)
    c_i = lax.broadcasted_iota(jnp.int32, (r, r), 1)
    tril = ((r_i >> shift) == (c_i >> shift)) & (r_i >= c_i)
    tril16 = jnp.where(tril, 1.0, 0.0).astype(BF16)
    lane_chunk = lax.broadcasted_iota(jnp.int32, (DK_REC, r), 1) >> shift

    def rows_to_cols(row):
        return jnp.broadcast_to(row, (row.shape[1], row.shape[1])).T

    if carry:
        @pl.when(t == 0)
        def _():
            for h in range(H_REC):
                st_scr[h] = s0_ref[0, h]

    heads = range(H_REC)
    cols = [slice(h * DK_REC, (h + 1) * DK_REC) for h in heads]
    lf = [lf_ref[0, :, cols[h]] for h in heads]
    v16 = [v_ref[0, :, cols[h]] for h in heads]
    b = []
    for h in heads:
        hi = lf[h].astype(BF16)
        lo = (lf[h] - hi.astype(F32)).astype(BF16)
        b.append(jnp.dot(tril16, hi, preferred_element_type=F32) + jnp.dot(tril16, lo, preferred_element_type=F32))
    qe, qe16, ke16, kd, dec = [], [], [], [], []
    for h in heads:
        b3 = b[h].reshape(nc, c, DK_REC)
        bl = b3[:, c - 1:c, :]
        k = 1.0 - jnp.exp(lf[h])
        qe.append(q_ref[0, :, cols[h]].astype(F32) * jnp.exp(b[h]))
        qe16.append(qe[h].astype(BF16))
        ke16.append((k * jnp.exp(-b[h])).astype(BF16))
        kd.append((k.reshape(nc, c, DK_REC) * jnp.exp(bl - b3)).reshape(r, DK_REC))
        dec.append(jnp.exp(bl))
    a = [lax.dot_general(qe16[h], ke16[h], _NT, preferred_element_type=F32) for h in heads]
    u = []
    for h in heads:
        kdt = kd[h].T
        lhs = jnp.concatenate([jnp.where(lane_chunk == n, kdt, 0.0) for n in range(nc)], axis=0).astype(BF16)
        u.append(jnp.dot(lhs, v16[h], preferred_element_type=F32))
    o = [jnp.dot(jnp.where(tril, a[h], 0.0).astype(BF16), v16[h], preferred_element_type=F32) for h in heads]
    qe_mm = qe16 if mm == BF16 else qe
    parts = [[] for _ in heads]
    st = [st_scr[h] for h in heads] if carry else None
    for n in range(nc):
        rows = slice(n * c, (n + 1) * c)
        for h in heads:
            s_in = st[h] if carry else s0_ref[n, h]
            parts[h].append(jnp.dot(qe_mm[h][rows], s_in.astype(mm), preferred_element_type=F32))
            s_out = s_in * rows_to_cols(dec[h][n]) + u[h][n * DK_REC:(n + 1) * DK_REC]
            if carry:
                st[h] = s_out
            else:
                s_ref[n, h] = s_out
    for h in heads:
        if carry:
            st_scr[h] = st[h]
        oh = o[h] + jnp.concatenate(parts[h], axis=0)
        on = oh * lax.rsqrt(jnp.mean(oh * oh, axis=-1, keepdims=True) + EPS) * w_ref[...]
        y_ref[0, :, cols[h]] = (on * g_ref[0, :, cols[h]].astype(F32)).astype(BF16)

    if carry:
        @pl.when(t == pl.num_programs(1) - 1)
        def _():
            for h in range(H_REC):
                s_ref[0, h] = st_scr[h]


def _hgrn(q, lf, v, g, s0, w_row, rows, c, carry):
    gdim, t, _ = q.shape
    nseq = 1 if carry else rows // c
    blk = pl.BlockSpec((1, rows, W_REC), lambda gi, ti: (gi, ti, 0))
    if carry:
        sblk = pl.BlockSpec((1, H_REC, DK_REC, DV_REC), lambda gi, ti: (gi, 0, 0, 0))
    else:
        sblk = pl.BlockSpec((nseq, H_REC, DK_REC, DV_REC), lambda gi, ti: (ti, 0, 0, 0))
    return pl.pallas_call(
        functools.partial(_hgrn_kernel, c=c, carry=carry),
        grid=(gdim, t // rows),
        in_specs=[blk, blk, blk, blk, sblk, _const_spec((1, DV_REC))],
        out_specs=[blk, sblk],
        out_shape=[jax.ShapeDtypeStruct((gdim, t, W_REC), BF16), jax.ShapeDtypeStruct(s0.shape, F32)],
        scratch_shapes=[pltpu.VMEM((H_REC, DK_REC, DV_REC), F32)],
        compiler_params=pltpu.CompilerParams(dimension_semantics=("parallel", "arbitrary"),
                                             vmem_limit_bytes=V7X_VMEM_LIMIT_BYTES),
        name="hgrn",
    )(q, lf, v, g, s0, w_row)


def _rel_bucket(dist):
    dist = jnp.maximum(dist, 0)
    max_exact = NUM_BUCKETS // 2
    large = max_exact + (jnp.log(jnp.maximum(dist, 1).astype(F32) / max_exact)
                         / math.log(MAX_DISTANCE / max_exact) * (NUM_BUCKETS - max_exact)).astype(jnp.int32)
    large = jnp.minimum(large, NUM_BUCKETS - 1)
    return jnp.where(dist < max_exact, dist, large)


def _shifted_bias(rel_bias, dist):
    rb = rel_bias.astype(F32)
    rb = (rb - rb[NUM_BUCKETS - 1]) * LOG2E
    bucket = _rel_bucket(dist)[None]
    expand = (slice(None),) + (None,) * dist.ndim
    out = jnp.zeros((rb.shape[1],) + dist.shape, F32)
    for kb in range(NUM_BUCKETS - 1):
        out = jnp.where(bucket == kb, rb[kb][expand], out)
    return jnp.where((dist >= 0)[None], out, NEG)


def _split_halves(q):
    lane = lax.broadcasted_iota(jnp.int32, q.shape, 1)
    zero = jnp.zeros_like(q)
    return jnp.concatenate([jnp.where(lane < DK_ATT, q, zero), jnp.where(lane >= DK_ATT, q, zero)], axis=0)


def _attn_prompt_kernel(lam_ref, q_ref, k_ref, vt_ref, bd_ref, bs_ref, w_ref, o_ref, m_scr, l_scr, acc_scr, *,
                        tq, lam_init):
    qi = pl.program_id(1)
    for h0 in range(0, H_ATT, ATT_HEAD_GROUP):
        _attn_prompt_heads(range(h0, h0 + ATT_HEAD_GROUP), qi, lam_ref, q_ref, k_ref, vt_ref, bd_ref, bs_ref,
                           w_ref, o_ref, m_scr, l_scr, acc_scr, tq, lam_init)


def _attn_prompt_heads(heads, qi, lam_ref, q_ref, k_ref, vt_ref, bd_ref, bs_ref, w_ref, o_ref, m_scr, l_scr,
                       acc_scr, tq, lam_init):
    nt = 2 * tq // ATT_TILE
    cols = {h: slice(h * DV_ATT, (h + 1) * DV_ATT) for h in heads}
    q_tiles = {}
    for h in heads:
        qs = _split_halves(q_ref[0, :, cols[h]])
        q_tiles[h] = [qs[i * ATT_TILE:(i + 1) * ATT_TILE] for i in range(nt)]

    def scores(j):
        out = []
        for h in heads:
            k_blk = k_ref[0, pl.ds(pl.multiple_of(j * tq, tq), tq), cols[h]]
            out.append(tuple(lax.dot_general(k_blk, q_tiles[h][i], _NT, preferred_element_type=F32)
                             for i in range(nt)))
        return tuple(out)

    def update(j, s_all, bias_ref, extra, first):
        for g, h in enumerate(heads):
            vt_blk = vt_ref[j, cols[h], :]
            for i in range(nt):
                slot = h * nt + i
                s = s_all[g][i]
                if bias_ref is not None:
                    c0 = (i * ATT_TILE) % tq
                    s = s + bias_ref[h, :, c0:c0 + ATT_TILE]
                    if extra is not None:
                        s = s + extra
                cmax = jnp.max(s, axis=0, keepdims=True)
                m_new = cmax if first else jnp.maximum(m_scr[slot], cmax)
                p = jnp.exp2(s - m_new)
                psum = jnp.sum(p, axis=0, keepdims=True)
                pv = jnp.dot(vt_blk, p.astype(BF16), preferred_element_type=F32)
                if first:
                    l_scr[slot] = psum
                    acc_scr[slot] = pv
                else:
                    alpha = jnp.exp2(m_scr[slot] - m_new)
                    l_scr[slot] = l_scr[slot] * alpha + psum
                    acc_scr[slot] = acc_scr[slot] * alpha + pv
                m_scr[slot] = m_new

    sub_j = jnp.maximum(qi - 1, 0)
    n_far = jnp.maximum(qi - 1, 0)
    s_diag = scores(qi)
    s_sub = scores(sub_j)
    update(qi, s_diag, bd_ref, None, True)
    update(sub_j, s_sub, bs_ref, jnp.where(qi == 0, NEG, 0.0), False)

    @pl.loop(0, n_far)
    def _(j):
        update(j, scores(j), None, None, False)

    for g, h in enumerate(heads):
        o_t = jnp.concatenate([acc_scr[h * nt + i] / l_scr[h * nt + i] for i in range(nt)], axis=1)
        o_t = o_t[:, :tq] - lam_ref[0] * o_t[:, tq:]
        o_t = o_t * lax.rsqrt(jnp.mean(o_t * o_t, axis=0, keepdims=True) + EPS)
        o_ref[0, :, cols[h]] = (o_t.T * w_ref[...] * (1.0 - lam_init)).astype(BF16)


def _attn_prompt(lam, qa, k16, vt, bias_diag, bias_sub, w_row, tq, lam_init):
    b, s, _ = qa.shape
    nb = s // tq
    assert tq >= MAX_DISTANCE and s % tq == 0
    n_slots = H_ATT * 2 * tq // ATT_TILE
    return pl.pallas_call(
        functools.partial(_attn_prompt_kernel, tq=tq, lam_init=lam_init),
        grid=(b, nb),
        in_specs=[pl.BlockSpec(memory_space=pltpu.SMEM),
                  pl.BlockSpec((1, tq, W_ATT), lambda bi, qi: (bi, qi, 0)),
                  pl.BlockSpec((1, s, W_ATT), lambda bi, qi: (bi, 0, 0)),
                  pl.BlockSpec((nb, W_ATT, tq), lambda bi, qi: (bi, 0, 0)),
                  _const_spec((H_ATT, tq, tq)), _const_spec((H_ATT, tq, tq)), _const_spec((1, DV_ATT))],
        out_specs=pl.BlockSpec((1, tq, W_ATT), lambda bi, qi: (bi, qi, 0)),
        out_shape=jax.ShapeDtypeStruct((b, s, W_ATT), BF16),
        scratch_shapes=[pltpu.VMEM((n_slots, 1, ATT_TILE), F32), pltpu.VMEM((n_slots, 1, ATT_TILE), F32),
                        pltpu.VMEM((n_slots, DV_ATT, ATT_TILE), F32)],
        compiler_params=pltpu.CompilerParams(dimension_semantics=("parallel", "arbitrary")),
        name="attn_prompt",
    )(lam, qa, k16, vt, bias_diag, bias_sub, w_row)


def _attn_sample_kernel(pt_ref, lam_ref, wp_ref, kn_ref, vn_ref, bnew_ref, btail_ref, w_ref, ck_hbm, cv_hbm,
                        o_ref, kbuf, vbuf, sem, m_scr, l_scr, acc_scr, *, t_new, lam_init):
    npg = PAGES_PER_STEP
    ahead = PAGE_SLOTS - 1
    j = pl.program_id(1)
    n_steps = pl.num_programs(1)
    g = pl.program_id(0) * n_steps + j
    last = pl.num_programs(0) * n_steps - 1
    slot = lax.rem(g, PAGE_SLOTS)
    rh = 2 * t_new
    half = H_ATT * rh
    ncol = 2 * half
    q_cols = wp_ref[0]
    lane = lax.broadcasted_iota(jnp.int32, q_cols.shape, 1)
    wpair = jnp.concatenate(
        [jnp.where((lane >= par * half + h * rh) & (lane < par * half + (h + 1) * rh), q_cols, jnp.zeros_like(q_cols))
         for par in range(2) for h in range(H_ATT)], axis=0)

    def page_copies(step, slot_):
        row, col0 = lax.div(step, n_steps), lax.rem(step, n_steps) * npg
        copies = []
        for i in range(npg):
            page = pt_ref[row, col0 + i]
            copies.append(pltpu.make_async_copy(ck_hbm.at[page], kbuf.at[slot_, i], sem.at[slot_]))
            copies.append(pltpu.make_async_copy(cv_hbm.at[page], vbuf.at[slot_, i], sem.at[slot_]))
        return copies

    def later_step(d):
        return jnp.minimum(g + d, last), lax.rem(g + d, PAGE_SLOTS)

    @pl.when(g == 0)
    def _():
        for d in range(ahead):
            for c in page_copies(*later_step(d)):
                c.start()

    for c in page_copies(g, slot):
        c.wait()
    k_refs = [kbuf.at[slot, i] for i in range(npg)]
    v_refs = [vbuf.at[slot, i] for i in range(npg)]

    def head_rows(ref, h):
        return ref[pl.ds(h, PAGE_SIZE, stride=H_ATT), :].astype(BF16)

    def rows_to_cols(row):
        return jnp.broadcast_to(row, (row.shape[1], row.shape[1])).T

    def weighted_values(p_t, get_v, parity):
        return [jnp.dot(p_t[parity * half + h * rh:parity * half + (h + 1) * rh], get_v(h),
                        preferred_element_type=F32) for h in range(H_ATT)]

    @pl.when(j == 0)
    def _():
        pad = PAGE_SIZE - t_new
        k_new = jnp.concatenate([kn_ref[0], jnp.zeros((pad, W_ATT), BF16)], axis=0)
        v_new = jnp.concatenate([vn_ref[0], jnp.zeros((pad, W_ATT), BF16)], axis=0)
        s = jnp.dot(k_new, wpair[:W_ATT, :], preferred_element_type=F32) + bnew_ref[...]
        m = jnp.max(s, axis=0, keepdims=True)
        even = lax.broadcasted_iota(jnp.int32, s.shape, 1) < half
        p = jnp.where(even, jnp.exp2(s - m), 0.0)
        m_scr[...] = jnp.where(even[:1], m, NEG)
        l_scr[...] = jnp.sum(p, axis=0, keepdims=True)
        p_t = p.T.astype(BF16)
        acc_scr[...] = jnp.concatenate(
            weighted_values(p_t, lambda h: v_new[:, h * DV_ATT:(h + 1) * DV_ATT], 0)
            + [jnp.zeros((half, DV_ATT), F32)], axis=0)

    for c in page_copies(*later_step(ahead)):
        c.start()
    tiles = []
    for pi in range(npg // 2):
        lhs = jnp.concatenate([head_rows(k_refs[2 * pi + par], h) for par in range(2) for h in range(H_ATT)],
                              axis=1)
        s = jnp.dot(lhs, wpair, preferred_element_type=F32)
        if pi == npg // 2 - 1:
            s = s + btail_ref[0]
        tiles.append(s)
    mx = tiles[0]
    for s in tiles[1:]:
        mx = jnp.maximum(mx, s)
    m_old = m_scr[...]
    m_new = jnp.maximum(m_old, jnp.max(mx, axis=0, keepdims=True))
    alpha = jnp.exp2(m_old - m_new)
    acc = acc_scr[...] * rows_to_cols(alpha)
    lsum = jnp.zeros_like(alpha)
    for pi in range(npg // 2):
        p = jnp.exp2(tiles[pi] - m_new)
        lsum = lsum + jnp.sum(p, axis=0, keepdims=True)
        p_t = p.T.astype(BF16)
        acc = acc + jnp.concatenate(
            [pv for par in range(2)
             for pv in weighted_values(p_t, functools.partial(head_rows, v_refs[2 * pi + par]), par)], axis=0)
    l = l_scr[...] * alpha + lsum
    m_scr[...] = m_new
    l_scr[...] = l
    acc_scr[...] = acc

    @pl.when(j == pl.num_programs(1) - 1)
    def _():
        m_c, l_c = rows_to_cols(m_new), rows_to_cols(l)
        m_f = jnp.maximum(m_c[:half], m_c[half:])
        w_e, w_o = jnp.exp2(m_c[:half] - m_f), jnp.exp2(m_c[half:] - m_f)
        o = (acc[:half] * w_e + acc[half:] * w_o) / (l_c[:half] * w_e + l_c[half:] * w_o)
        for h in range(H_ATT):
            oh = o[h * rh:h * rh + t_new] - lam_ref[0] * o[h * rh + t_new:(h + 1) * rh]
            oh = oh * lax.rsqrt(jnp.mean(oh * oh, axis=-1, keepdims=True) + EPS) * w_ref[...] * (1.0 - lam_init)
            o_ref[0, :, h * DV_ATT:(h + 1) * DV_ATT] = oh.astype(BF16)

    @pl.when(g == last)
    def _():
        for d in range(1, PAGE_SLOTS):
            for c in page_copies(*later_step(d)):
                c.wait()


def _attn_sample(page_table, lam, wpair, k_new, v_new, bias_new, bias_tail, w_row, cache_k, cache_v,
                 t_new, lam_init):
    nb, n_pages = page_table.shape
    npg = PAGES_PER_STEP
    assert n_pages % npg == 0 and npg % 2 == 0
    n_steps = n_pages // npg
    ncol = wpair.shape[2]
    prow = PAGE_SIZE * H_ATT
    per_b = lambda n, w: pl.BlockSpec((1, n, w), lambda b, j, pt: (b, 0, 0))
    const = lambda shape: pl.BlockSpec(shape, lambda b, j, pt: (0,) * len(shape))
    hbm = pl.BlockSpec(memory_space=pl.ANY)
    grid_spec = pltpu.PrefetchScalarGridSpec(
        num_scalar_prefetch=1,
        grid=(nb, n_steps),
        in_specs=[pl.BlockSpec(memory_space=pltpu.SMEM), per_b(DV_ATT, ncol), per_b(t_new, W_ATT),
                  per_b(t_new, W_ATT), const((PAGE_SIZE, ncol)),
                  pl.BlockSpec((1, PAGE_SIZE, ncol), lambda b, j, pt: (jnp.where(j == n_steps - 1, 1, 0), 0, 0)),
                  const((1, DV_ATT)), hbm, hbm],
        out_specs=pl.BlockSpec((1, t_new, W_ATT), lambda b, j, pt: (b, 0, 0)),
        scratch_shapes=[pltpu.VMEM((PAGE_SLOTS, npg, prow, DV_ATT), F32),
                        pltpu.VMEM((PAGE_SLOTS, npg, prow, DV_ATT), F32),
                        pltpu.SemaphoreType.DMA((PAGE_SLOTS,)),
                        pltpu.VMEM((1, ncol), F32), pltpu.VMEM((1, ncol), F32), pltpu.VMEM((ncol, DV_ATT), F32)],
    )
    return pl.pallas_call(
        functools.partial(_attn_sample_kernel, t_new=t_new, lam_init=lam_init),
        grid_spec=grid_spec,
        out_shape=jax.ShapeDtypeStruct((nb, t_new, W_ATT), BF16),
        compiler_params=pltpu.CompilerParams(dimension_semantics=("arbitrary", "arbitrary"),
                                             vmem_limit_bytes=V7X_VMEM_LIMIT_BYTES),
        name="attn_sample",
    )(page_table, lam, wpair, k_new, v_new, bias_new, bias_tail, w_row, cache_k, cache_v)


def _out_mlp_kernel(x_ref, yr_ref, ya_ref, ga_ref, gb_ref, wa_ref, wb_ref, wo_ref, n2_ref, wu_ref, wd_ref,
                    y_ref, *, ff_chunk):
    a = jnp.dot(yr_ref[...], wa_ref[...], preferred_element_type=F32)
    b = jnp.dot(ya_ref[...], wb_ref[...], preferred_element_type=F32)
    mixed = ga_ref[...].astype(F32) * a + gb_ref[...].astype(F32) * b
    h = x_ref[...] + jnp.dot(mixed.astype(BF16), wo_ref[...], preferred_element_type=F32)
    ms = jnp.mean(h * h, axis=-1, keepdims=True)
    hn = (h * lax.rsqrt(ms + EPS) * n2_ref[...]).astype(BF16)
    y = h
    for c0 in range(0, D_FF, ff_chunk):
        u = jnp.maximum(jnp.dot(hn, wu_ref[:, c0:c0 + ff_chunk], preferred_element_type=F32), 0.0)
        y = y + jnp.dot((u * u).astype(BF16), wd_ref[c0:c0 + ff_chunk, :], preferred_element_type=F32)
    y_ref[...] = y


def _out_mlp(x2d, y_rec, y_att, ga, gb, wa16, wb16, wo16, n2, wu16, wd16, tm):
    m = x2d.shape[0]
    row = lambda w: pl.BlockSpec((tm, w), lambda i: (i, 0))
    return pl.pallas_call(
        functools.partial(_out_mlp_kernel, ff_chunk=D_MODEL),
        grid=(m // tm,),
        in_specs=[row(D_MODEL), row(W_REC), row(W_ATT), row(D_MODEL), row(D_MODEL),
                  _const_spec((W_REC, D_MODEL)), _const_spec((W_ATT, D_MODEL)), _const_spec((D_MODEL, D_MODEL)),
                  _const_spec((1, D_MODEL)), _const_spec((D_MODEL, D_FF)), _const_spec((D_FF, D_MODEL))],
        out_specs=row(D_MODEL),
        out_shape=jax.ShapeDtypeStruct((m, D_MODEL), F32),
        compiler_params=pltpu.CompilerParams(dimension_semantics=("parallel",),
                                             vmem_limit_bytes=V7X_VMEM_LIMIT_BYTES),
        name="out_mlp",
    )(x2d, y_rec, y_att, ga, gb, wa16, wb16, wo16, n2, wu16, wd16)


def kernel(x_prompt, x_sample, cache_k, cache_v, state_hgrn, page_table, norm1_w, w_in, lower_bounds, hgrn_norm_w, q_norm_w, k_norm_w, lam_q1, lam_k1, lam_q2, lam_k2, subln_w, rel_bias, w_branch_a, w_branch_b, w_out, norm2_w, w_up, w_down):
    depth = w_in.shape[0]
    assert depth == 1
    bp, sp, _ = x_prompt.shape
    bs, ts, _ = x_sample.shape
    n_pages = page_table.shape[1]
    past = n_pages * PAGE_SIZE
    assert PAGE_SIZE >= MAX_DISTANCE - 1 and ts <= HGRN_CHUNK
    l = 0
    lam_init = 0.8 - 0.6 * math.exp(-0.3 * l)
    scale = DK_ATT ** -0.5

    lb = jnp.cumsum(jax.nn.softmax(lower_bounds.astype(F32), axis=0), axis=0)[l][None, :]
    lam = (jnp.exp(jnp.dot(lam_q1[l].astype(F32), lam_k1[l].astype(F32)))
           - jnp.exp(jnp.dot(lam_q2[l].astype(F32), lam_k2[l].astype(F32))) + lam_init).reshape(1)
    n1 = norm1_w[l].astype(F32)[None, :]
    n2 = norm2_w[l].astype(F32)[None, :]
    qn_row = jnp.tile(q_norm_w[l].astype(F32).reshape(-1), H_ATT)[None, :] * (scale * LOG2E)
    kn_row = jnp.tile(k_norm_w[l].astype(F32).reshape(-1), H_ATT)[None, :]
    hw_row = hgrn_norm_w[l].astype(F32)[None, :]
    sw_row = subln_w[l].astype(F32)[None, :]
    seg = jnp.arange(W_ATT) // DK_ATT
    pseg = jnp.where(seg[:, None] == seg[None, :], 1.0 / DK_ATT, 0.0).astype(BF16)
    w_in16, wa16, wb16 = w_in[l].astype(BF16), w_branch_a[l].astype(BF16), w_branch_b[l].astype(BF16)
    wo16, wu16, wd16 = w_out[l].astype(BF16), w_up[l].astype(BF16), w_down[l].astype(BF16)

    def layer(x, attend, hgrn_fn, tm, kb):
        b, t, _ = x.shape
        x2d = x.reshape(b * t, D_MODEL)
        q, lf, vr, g, qa, k32, k16, v32, vt, ga, gb = _in_proj(x2d, n1, w_in16, lb, qn_row, kn_row, pseg, tm, kb)
        y_rec, s_new = hgrn_fn(q, lf, vr, g)
        y_att = attend(qa, k16, vt)
        y = _out_mlp(x2d, y_rec.reshape(b * t, W_REC), y_att.reshape(b * t, W_ATT), ga, gb,
                     wa16, wb16, wo16, n2, wu16, wd16, tm)
        return (y.reshape(b, t, D_MODEL), k32.reshape(1, b, t, H_ATT, DV_ATT),
                v32.reshape(1, b, t, H_ATT, DV_ATT), s_new[None])

    tq = ATT_BLOCK
    ri = jnp.arange(tq)
    bias_diag = _shifted_bias(rel_bias, ri[None, :] - ri[:, None])
    bias_sub = _shifted_bias(rel_bias, ri[None, :] + tq - ri[:, None])
    r3p = lambda a: a.reshape(bp, sp, a.shape[-1])
    attend_p = lambda qa, k16, vt: _attn_prompt(lam, r3p(qa), r3p(k16), vt, bias_diag, bias_sub, sw_row, tq, lam_init)
    s0p = jnp.zeros((bp, H_REC, DK_REC, DV_REC), F32)
    hgrn_p = lambda q, lf, vr, g: _hgrn(r3p(q), r3p(lf), r3p(vr), r3p(g), s0p, hw_row, 256, HGRN_CHUNK, True)
    y_p, k_p, v_p, s_p = layer(x_prompt, attend_p, hgrn_p, 512, tq)

    half = H_ATT * 2 * ts
    col_h = jnp.arange(half) // (2 * ts)
    col_t = jnp.arange(half) % ts
    assert 2 * half == V7X_LANES and n_pages % 2 == 0

    def own_head(bias):
        out = jnp.zeros(bias.shape[1:], F32)
        for h in range(H_ATT):
            out = jnp.where((col_h == h)[None, :], bias[h], out)
        return out

    keys = jnp.arange(PAGE_SIZE)
    tail_dist = (past + col_t)[None, :] - ((n_pages - 1) * PAGE_SIZE + keys)[:, None]
    zeros_half = jnp.zeros((PAGE_SIZE, half), F32)
    bias_tail = jnp.stack([jnp.zeros((PAGE_SIZE, 2 * half), F32),
                           jnp.concatenate([zeros_half, own_head(_shifted_bias(rel_bias, tail_dist))], axis=1)])
    new_dist = jnp.where((keys < ts)[:, None], col_t[None, :] - keys[:, None], -1)
    bias_new = jnp.concatenate([own_head(_shifted_bias(rel_bias, new_dist)), zeros_half], axis=1)
    ck = cache_k[l].reshape(cache_k.shape[1], PAGE_SIZE * H_ATT, DV_ATT)
    cv = cache_v[l].reshape(cache_v.shape[1], PAGE_SIZE * H_ATT, DV_ATT)

    def attend_s(qa, k16, vt):
        q4 = qa.reshape(bs, ts, H_ATT, DV_ATT).transpose(0, 3, 2, 1)
        d = jnp.arange(DV_ATT)[:, None, None]
        qcols = jnp.concatenate([jnp.where(d < DK_ATT, q4, 0), jnp.where(d >= DK_ATT, q4, 0)], axis=3)
        qcols = qcols.reshape(bs, DV_ATT, half)
        return _attn_sample(page_table, lam, jnp.concatenate([qcols, qcols], axis=2).astype(BF16),
                            k16.reshape(bs, ts, W_ATT), vt.reshape(bs, ts, W_ATT),
                            bias_new, bias_tail, sw_row, ck, cv, ts, lam_init)

    r3s = lambda a: a.reshape(1, bs * ts, a.shape[-1])
    hgrn_rows = 16 * ts
    hgrn_s = lambda q, lf, vr, g: _hgrn(r3s(q), r3s(lf), r3s(vr), r3s(g), state_hgrn[l].astype(F32), hw_row,
                                        hgrn_rows, ts, False)
    y_s, k_s, v_s, s_s = layer(x_sample, attend_s, hgrn_s, 512, None)
    return (y_p, y_s, k_p, v_p, s_p, k_s, v_s, s_s)
```
